```python
import jax, jax.numpy as jnp
from jax import lax
import numpy as np

D_MODEL = 1024
BATCH = 4
SEQ = 4096
DEPTH = 4
DEC_BATCH = 128
DEC_SEQ = 8
PAST_LEN = 2048
PAGE_SIZE = 128

N_A = DEPTH // 2
N_B = DEPTH - N_A
D_RNN = 768
N_RG_BLOCKS = 8
RG_BW = D_RNN // N_RG_BLOCKS
CONV_W = 4
RG_C = 8.0
N_HEADS = 12
N_KV = 4
HEAD_DIM = 64
CMP_LEN = 32
CMP_STRIDE = 16
CMP_HID = 128
SEL_BLK = 64
N_SEL = 16
WINDOW = 512
Q_BLK = 64
N_MEM = 256
MEM_HEADS = 4
MEM_HD = 64
MEM_W = MEM_HEADS * MEM_HD
N_EXPERTS = 32
TOP_K = 4
D_EXPERT = 1024
SWIGLU_LIMIT = 7.0
SWIGLU_ALPHA = 1.702
MOE_BLK = 256
DN_ALPHA = (2 * DEPTH) ** 0.25
DN_BETA = (8 * DEPTH) ** -0.25
LN_EPS = 1e-5

kernel_name = 'hawk_nsa_yoco_moe_step'

F32 = jnp.float32


def layer_norm(x, g, b):
    xf = x.astype(F32)
    mu = jnp.mean(xf, -1, keepdims=True)
    var = jnp.mean(jnp.square(xf - mu), -1, keepdims=True)
    y = (xf - mu) * lax.rsqrt(var + LN_EPS)
    return (y * g.astype(F32) + b.astype(F32)).astype(x.dtype)


def masked_softmax(s, mask):
    s = jnp.where(mask, s, -jnp.inf)
    m = jnp.max(s, axis=-1, keepdims=True)
    m = jnp.where(jnp.isfinite(m), m, 0.0)
    p = jnp.exp(s - m)
    d = jnp.sum(p, axis=-1, keepdims=True)
    return p / jnp.where(d > 0, d, 1.0)


def mem_attend(q, mem_kv):
    B, T, _ = q.shape
    qh = q.reshape(B, T, MEM_HEADS, MEM_HD)
    k, v = mem_kv[:, :, 0], mem_kv[:, :, 1]
    s = jnp.einsum('bthd,bmhd->bhtm', qh, k).astype(F32) * (MEM_HD ** -0.5)
    p = jax.nn.softmax(s, axis=-1)
    o = jnp.einsum('bhtm,bmhd->bthd', p.astype(v.dtype), v)
    return o.reshape(B, T, MEM_W)


def _lin_combine(c1, c2):
    a1, b1 = c1
    a2, b2 = c2
    return a1 * a2, a2 * b1 + b2


def rglru_branch(u, conv_buf, h0, conv_w, conv_b, w_a, b_a, w_i, b_i, lam):
    B, T, C = u.shape
    full = jnp.concatenate([conv_buf.astype(u.dtype), u], axis=1)
    xc = lax.conv_general_dilated(full, conv_w.astype(u.dtype)[:, None, :], (1,), 'VALID',
                                  dimension_numbers=('NWC', 'WIO', 'NWC'),
                                  feature_group_count=C) + conv_b
    xb = xc.reshape(B, T, N_RG_BLOCKS, RG_BW)
    r = jax.nn.sigmoid(jnp.einsum('btnc,ncd->btnd', xb, w_a).reshape(B, T, C) + b_a)
    i = jax.nn.sigmoid(jnp.einsum('btnc,ncd->btnd', xb, w_i).reshape(B, T, C) + b_i)
    log_a = -RG_C * r.astype(F32) * jax.nn.softplus(-lam.astype(F32))
    a = jnp.exp(log_a)
    b = jnp.sqrt(-jnp.expm1(2.0 * log_a)) * (i * xc).astype(F32)
    b = b.at[:, 0].add(a[:, 0] * h0.astype(F32))
    _, h = lax.associative_scan(_lin_combine, (a, b), axis=1)
    return h.astype(u.dtype), h[:, -1].astype(u.dtype), full[:, T:]


def shared_kv(x, w_kv):
    B, T, _ = x.shape
    return jnp.einsum('btd,de->bte', x, w_kv).reshape(B, T, 6, N_KV, HEAD_DIM)


def compress_tokens(kseq, pe, w1, b1, w2, b2):
    B, T, G, Dh = kseq.shape
    R = CMP_LEN // CMP_STRIDE
    n_chunk = T // CMP_STRIDE
    n_cmp = n_chunk - R + 1
    ch = kseq[:, :n_chunk * CMP_STRIDE].reshape(B, n_chunk, CMP_STRIDE, G, Dh)
    pe_r = pe.reshape(R, CMP_STRIDE, Dh)
    w1_r = w1.reshape(R, CMP_STRIDE, Dh, CMP_HID)
    hid = b1
    for r in range(R):
        hid = hid + jnp.einsum('bnsgd,sdh->bngh', ch[:, r:r + n_cmp] + pe_r[r][:, None, :], w1_r[r])
    out = jnp.einsum('bngh,hd->bngd', jax.nn.gelu(hid), w2) + b2
    ends = jnp.arange(n_cmp) * CMP_STRIDE + (CMP_LEN - 1)
    return out, ends


def nsa_context(kv4, cmp_pe, cmp_w1, cmp_b1, cmp_w2, cmp_b2):
    kc, ends = compress_tokens(kv4[:, :, 0], cmp_pe[0], cmp_w1[0], cmp_b1[0], cmp_w2[0], cmp_b2[0])
    vc, _ = compress_tokens(kv4[:, :, 1], cmp_pe[1], cmp_w1[1], cmp_b1[1], cmp_w2[1], cmp_b2[1])
    B, T = kv4.shape[:2]
    n_sel = -(-T // SEL_BLK)
    sel = jnp.pad(kv4[:, :, 2:4], ((0, 0), (0, n_sel * SEL_BLK - T), (0, 0), (0, 0), (0, 0)))
    sel = sel.reshape(B, n_sel, SEL_BLK, 2, N_KV, HEAD_DIM).transpose(3, 0, 4, 1, 2, 5)
    return kc, vc, ends, sel[0], sel[1]


def nsa_attend(q, gates, q_pos, kc, vc, c_end, ksb, vsb, kw, vw, w_pos):
    B, Qb, H, Dh = q.shape
    G = kc.shape[2]
    R = H // G
    n_sel = ksb.shape[2]
    scale = Dh ** -0.5
    qg = q.reshape(B, Qb, G, R, Dh)
    s_c = jnp.einsum('bqgrd,bngd->bgrqn', qg, kc).astype(F32) * scale
    p_c = masked_softmax(s_c, c_end[None, :] <= q_pos[:, None])
    o_c = jnp.einsum('bgrqn,bngd->bqgrd', p_c.astype(vc.dtype), vc)
    c_start = c_end - (CMP_LEN - 1)
    s_start = jnp.arange(n_sel) * SEL_BLK
    cover = ((c_start[:, None] < s_start[None, :] + SEL_BLK) & (c_end[:, None] >= s_start[None, :])).astype(F32)
    imp = jnp.einsum('bgrqn,nj->bgqj', p_c, cover)
    cur = q_pos // SEL_BLK
    j = jnp.arange(n_sel)
    forced = (j[None, :] == 0) | (j[None, :] == cur[:, None]) | (j[None, :] == cur[:, None] - 1)
    future = j[None, :] > cur[:, None]
    imp = jnp.where(forced, jnp.inf, jnp.where(future, -jnp.inf, imp))
    _, idx = lax.top_k(imp, min(N_SEL, n_sel))
    gather = jax.vmap(jax.vmap(lambda blocks, ids: blocks[ids]))
    ks = gather(ksb, idx)
    vs = gather(vsb, idx)
    n_top = idx.shape[-1]
    k_pos = idx[..., None] * SEL_BLK + jnp.arange(SEL_BLK)
    m_s = (k_pos <= q_pos[:, None, None]).reshape(B, G, 1, Qb, n_top * SEL_BLK)
    s_s = jnp.einsum('bqgrd,bgqksd->bgrqks', qg, ks).astype(F32) * scale
    p_s = masked_softmax(s_s.reshape(B, G, R, Qb, n_top * SEL_BLK), m_s)
    o_s = jnp.einsum('bgrqm,bgqmd->bqgrd', p_s.astype(vs.dtype), vs.reshape(B, G, Qb, n_top * SEL_BLK, Dh))
    dist = q_pos[:, None] - w_pos[None, :]
    m_w = (w_pos[None, :] >= 0) & (dist >= 0) & (dist <= WINDOW)
    s_w = jnp.einsum('bqgrd,bngd->bgrqn', qg, kw).astype(F32) * scale
    p_w = masked_softmax(s_w, m_w)
    o_w = jnp.einsum('bgrqn,bngd->bqgrd', p_w.astype(vw.dtype), vw)
    g = gates.reshape(B, Qb, G, R, 3)
    out = g[..., 0:1] * o_c + g[..., 1:2] * o_s + g[..., 2:3] * o_w
    return out.reshape(B, Qb, H * Dh)


def nsa_prompt(q, gates, ctx, k_win, v_win):
    B, T, H, Dh = q.shape
    n_blk = T // Q_BLK
    zpad = jnp.zeros((B, WINDOW) + k_win.shape[2:], k_win.dtype)
    kw_pad = jnp.concatenate([zpad, k_win], axis=1)
    vw_pad = jnp.concatenate([zpad.astype(v_win.dtype), v_win], axis=1)
    qb = q.reshape(B, n_blk, Q_BLK, H, Dh).swapaxes(0, 1)
    gb = gates.reshape(B, n_blk, Q_BLK, H, 3).swapaxes(0, 1)
    band = jnp.arange(WINDOW + Q_BLK)

    def block(args):
        qi, gi, i = args
        start = i * Q_BLK
        kw = lax.dynamic_slice_in_dim(kw_pad, start, WINDOW + Q_BLK, axis=1)
        vw = lax.dynamic_slice_in_dim(vw_pad, start, WINDOW + Q_BLK, axis=1)
        return nsa_attend(qi, gi, start + jnp.arange(Q_BLK), *ctx, kw, vw, start - WINDOW + band)

    out = lax.map(block, (qb, gb, jnp.arange(n_blk)))
    return out.swapaxes(0, 1).reshape(B, T, H * Dh)


def nsa_sample(q, gates, ctx, k_win, v_win, pos0, w_first):
    B, S, H, Dh = q.shape
    w_pos = w_first + jnp.arange(k_win.shape[1])
    q_pos = pos0 + jnp.arange(S)

    def step(args):
        qi, gi, t = args
        return nsa_attend(qi[:, None], gi[:, None], t[None], *ctx, k_win, v_win, w_pos)

    out = lax.map(step, (q.swapaxes(0, 1), gates.swapaxes(0, 1), q_pos))
    return out[:, :, 0].swapaxes(0, 1)


def moe_ffn(x, w_r, b_r, w_gu, b_gu, w_dn, b_dn):
    shp = x.shape
    xt = x.reshape(-1, shp[-1])
    N = xt.shape[0]
    logits = (xt @ w_r + b_r).astype(F32)
    top_v, top_e = lax.top_k(logits, TOP_K)
    wts = jax.nn.softmax(top_v, axis=-1)
    NK = N * TOP_K
    flat_e = top_e.reshape(-1)
    order = jnp.argsort(flat_e)
    e_sorted = flat_e[order]
    tok = order // TOP_K
    sizes = jnp.bincount(flat_e, length=N_EXPERTS)
    starts = jnp.cumsum(sizes) - sizes
    padded = (sizes + MOE_BLK - 1) // MOE_BLK * MOE_BLK
    pad_end = jnp.cumsum(padded)
    pad_start = pad_end - padded
    dest = pad_start[e_sorted] + jnp.arange(NK) - starts[e_sorted]
    n_blk = -(-NK // MOE_BLK) + N_EXPERTS
    x_pad = jnp.zeros((n_blk * MOE_BLK, shp[-1]), xt.dtype).at[dest].set(xt[tok])
    blk_e = jnp.minimum(jnp.searchsorted(pad_end, jnp.arange(n_blk) * MOE_BLK, side='right'), N_EXPERTS - 1)

    def expert_block(args):
        xb, e = args
        h = xb @ w_gu[e] + b_gu[e]
        gate = jnp.minimum(h[:, :D_EXPERT], SWIGLU_LIMIT)
        up = jnp.clip(h[:, D_EXPERT:], -SWIGLU_LIMIT, SWIGLU_LIMIT)
        act = (up + 1.0) * gate * jax.nn.sigmoid(SWIGLU_ALPHA * gate)
        return act @ w_dn[e] + b_dn[e]

    y_pad = lax.map(expert_block, (x_pad.reshape(n_blk, MOE_BLK, shp[-1]), blk_e)).reshape(-1, shp[-1])
    y_sorted = y_pad[dest]
    y_assign = jnp.zeros_like(y_sorted).at[order].set(y_sorted).reshape(N, TOP_K, shp[-1])
    out = jnp.einsum('nkd,nk->nd', y_assign, wts.astype(y_assign.dtype))
    return out.reshape(shp)


def setup_inputs(seed: int = 0) -> dict:
    key = jax.random.key(seed)
    keys = iter(jax.random.split(key, 48))

    def nrm(shape, scale):
        return jax.random.normal(next(keys), shape, jnp.float32) * scale

    n_pages = PAST_LEN // PAGE_SIZE
    n_used = DEC_BATCH * n_pages
    n_pool = n_used + n_used // 4
    page_table = jax.random.permutation(next(keys), n_pool)[:n_used].reshape(DEC_BATCH, n_pages).astype(jnp.int32)
    w_buf = min(WINDOW, PAST_LEN)
    hq = N_HEADS * HEAD_DIM
    u = jax.random.uniform(next(keys), (N_A, D_RNN), jnp.float32, 0.9, 0.999)
    a_base = u ** (1.0 / RG_C)
    rg_lambda = jnp.log(a_base) - jnp.log1p(-a_base)
    kv_col_scale = jnp.array([1.0, DN_BETA] * 3, jnp.float32)[None, :, None]
    mem_col_scale = jnp.array([1.0, DN_BETA], jnp.float32)[None, None, :, None]
    return {
        'x_prompt': nrm((BATCH, SEQ, D_MODEL), 1.0),
        'x_sample': nrm((DEC_BATCH, DEC_SEQ, D_MODEL), 1.0),
        'cache_kv': nrm((n_pool, PAGE_SIZE, 4, N_KV, HEAD_DIM), 1.0),
        'cache_win': nrm((DEC_BATCH, w_buf, 2, N_KV, HEAD_DIM), 1.0),
        'cache_mem_kv': nrm((DEPTH, DEC_BATCH, N_MEM, 2, MEM_HEADS, MEM_HD), 1.0),
        'state_rg_h': nrm((N_A, DEC_BATCH, D_RNN), 0.5),
        'state_rg_conv': nrm((N_A, DEC_BATCH, CONV_W - 1, D_RNN), 1.0),
        'page_table': page_table,
        'mem_prompt': nrm((BATCH, N_MEM, D_MODEL), 1.0),
        'ln_g': 1.0 + nrm((DEPTH, 2, D_MODEL), 0.02),
        'ln_b': nrm((DEPTH, 2, D_MODEL), 0.02),
        'w_in_a': nrm((N_A, D_MODEL, 2 * D_RNN + MEM_W), D_MODEL ** -0.5),
        'conv_w': nrm((N_A, CONV_W, D_RNN), CONV_W ** -0.5),
        'conv_b': nrm((N_A, D_RNN), 0.01),
        'w_rg_a': nrm((N_A, N_RG_BLOCKS, RG_BW, RG_BW), RG_BW ** -0.5),
        'b_rg_a': nrm((N_A, D_RNN), 0.01),
        'w_rg_i': nrm((N_A, N_RG_BLOCKS, RG_BW, RG_BW), RG_BW ** -0.5),
        'b_rg_i': nrm((N_A, D_RNN), 0.01),
        'rg_lambda': rg_lambda,
        'w_out_a': nrm((N_A, D_RNN + MEM_W, D_MODEL), (D_RNN + MEM_W) ** -0.5 * DN_BETA),
        'w_kv_shared': (nrm((D_MODEL, 6, N_KV * HEAD_DIM), D_MODEL ** -0.5) * kv_col_scale).reshape(D_MODEL, 6 * N_KV * HEAD_DIM),
        'cmp_pe': nrm((2, CMP_LEN, HEAD_DIM), 0.1),
        'cmp_w1': nrm((2, CMP_LEN * HEAD_DIM, CMP_HID), (CMP_LEN * HEAD_DIM) ** -0.5),
        'cmp_b1': nrm((2, CMP_HID), 0.01),
        'cmp_w2': nrm((2, CMP_HID, HEAD_DIM), CMP_HID ** -0.5),
        'cmp_b2': nrm((2, HEAD_DIM), 0.01),
        'w_in_b': nrm((N_B, D_MODEL, hq + 3 * N_HEADS + MEM_W), D_MODEL ** -0.5),
        'w_out_b': nrm((N_B, hq + MEM_W, D_MODEL), (hq + MEM_W) ** -0.5 * DN_BETA),
        'w_mem_kv': (nrm((DEPTH, D_MODEL, 2, MEM_W), D_MODEL ** -0.5) * mem_col_scale).reshape(DEPTH, D_MODEL, 2 * MEM_W),
        'w_router': nrm((DEPTH, D_MODEL, N_EXPERTS), D_MODEL ** -0.5),
        'b_router': nrm((DEPTH, N_EXPERTS), 0.01),
        'w_gate_up': nrm((DEPTH, N_EXPERTS, D_MODEL, 2 * D_EXPERT), D_MODEL ** -0.5),
        'b_gate_up': nrm((DEPTH, N_EXPERTS, 2 * D_EXPERT), 0.01),
        'w_down': nrm((DEPTH, N_EXPERTS, D_EXPERT, D_MODEL), D_EXPERT ** -0.5 * DN_BETA),
        'b_down': nrm((DEPTH, N_EXPERTS, D_MODEL), 0.01),
    }


def reference(x_prompt, x_sample, cache_kv, cache_win, cache_mem_kv, state_rg_h, state_rg_conv,
              page_table, mem_prompt, ln_g, ln_b, w_in_a, conv_w, conv_b, w_rg_a, b_rg_a,
              w_rg_i, b_rg_i, rg_lambda, w_out_a, w_kv_shared, cmp_pe, cmp_w1, cmp_b1, cmp_w2,
              cmp_b2, w_in_b, w_out_b, w_mem_kv, w_router, b_router, w_gate_up, b_gate_up,
              w_down, b_down):
    hq = N_HEADS * HEAD_DIM

    def post_norm(l, j, x, y):
        return layer_norm(DN_ALPHA * x + y, ln_g[l, j], ln_b[l, j])

    def channel(l, x):
        y = moe_ffn(x, w_router[l], b_router[l], w_gate_up[l], b_gate_up[l], w_down[l], b_down[l])
        return post_norm(l, 1, x, y)

    def layer_a(l, x, mem_kv, conv_buf, h0):
        p = jnp.einsum('btd,de->bte', x, w_in_a[l])
        gate = jax.nn.gelu(p[..., :D_RNN])
        u = p[..., D_RNN:2 * D_RNN]
        q_mem = p[..., 2 * D_RNN:]
        h, h_last, buf = rglru_branch(u, conv_buf, h0, conv_w[l], conv_b[l], w_rg_a[l], b_rg_a[l],
                                      w_rg_i[l], b_rg_i[l], rg_lambda[l])
        merged = jnp.concatenate([h * gate, mem_attend(q_mem, mem_kv)], axis=-1)
        x = post_norm(l, 0, x, jnp.einsum('bte,ed->btd', merged, w_out_a[l]))
        return channel(l, x), h_last, buf

    def layer_b(l, x, mem_kv, nsa_fn):
        jb = l - N_A
        B, T, _ = x.shape
        p = jnp.einsum('btd,de->bte', x, w_in_b[jb])
        q = p[..., :hq].reshape(B, T, N_HEADS, HEAD_DIM)
        gates = jax.nn.sigmoid(p[..., hq:hq + 3 * N_HEADS]).reshape(B, T, N_HEADS, 3)
        q_mem = p[..., hq + 3 * N_HEADS:]
        merged = jnp.concatenate([nsa_fn(q, gates), mem_attend(q_mem, mem_kv)], axis=-1)
        x = post_norm(l, 0, x, jnp.einsum('bte,ed->btd', merged, w_out_b[jb]))
        return channel(l, x)

    cmp_params = (cmp_pe, cmp_w1, cmp_b1, cmp_w2, cmp_b2)

    Bp, Tp, _ = x_prompt.shape
    mem_kv_prompt = jnp.einsum('bmd,lde->lbme', mem_prompt, w_mem_kv).reshape(
        DEPTH, Bp, N_MEM, 2, MEM_HEADS, MEM_HD)
    x = x_prompt
    hs_p, bufs_p = [], []
    for l in range(DEPTH):
        if l < N_A:
            x, h_last, buf = layer_a(l, x, mem_kv_prompt[l],
                                     jnp.zeros((Bp, CONV_W - 1, D_RNN), x.dtype),
                                     jnp.zeros((Bp, D_RNN), x.dtype))
            hs_p.append(h_last)
            bufs_p.append(buf)
        else:
            if l == N_A:
                kv_p = shared_kv(x, w_kv_shared)
                ctx_p = nsa_context(kv_p[:, :, :4], *cmp_params)
                nsa_fn_p = lambda q, g: nsa_prompt(q, g, ctx_p, kv_p[:, :, 4], kv_p[:, :, 5])
            x = layer_b(l, x, mem_kv_prompt[l], nsa_fn_p)
    y_prompt = x

    Bs, Ss, _ = x_sample.shape
    past_len = page_table.shape[1] * PAGE_SIZE
    w_buf = cache_win.shape[1]
    x = x_sample
    hs_s, bufs_s = [], []
    for l in range(DEPTH):
        if l < N_A:
            x, h_last, buf = layer_a(l, x, cache_mem_kv[l], state_rg_conv[l], state_rg_h[l])
            hs_s.append(h_last)
            bufs_s.append(buf)
        else:
            if l == N_A:
                kv_s = shared_kv(x, w_kv_shared)
                past = cache_kv[page_table].reshape(Bs, past_len, 4, N_KV, HEAD_DIM)
                ctx_s = nsa_context(jnp.concatenate([past, kv_s[:, :, :4]], axis=1), *cmp_params)
                win_s = jnp.concatenate([cache_win, kv_s[:, :, 4:]], axis=1)
                nsa_fn_s = lambda q, g: nsa_sample(q, g, ctx_s, win_s[:, :, 0], win_s[:, :, 1],
                                                   past_len, past_len - w_buf)
            x = layer_b(l, x, cache_mem_kv[l], nsa_fn_s)
    y_sample = x

    kv_rows_prompt = kv_p[:, :, :4]
    kv_rows_sample = kv_s[:, :, :4]
    win_prompt = kv_p[:, Tp - min(WINDOW, Tp):, 4:]
    win_sample = win_s[:, win_s.shape[1] - min(WINDOW, past_len + Ss):]
    rg_h_prompt = jnp.stack(hs_p)
    rg_h_sample = jnp.stack(hs_s)
    rg_conv_prompt = jnp.stack(bufs_p)
    rg_conv_sample = jnp.stack(bufs_s)
    return (y_prompt, y_sample, kv_rows_prompt, kv_rows_sample, win_prompt, win_sample,
            mem_kv_prompt, rg_h_prompt, rg_h_sample, rg_conv_prompt, rg_conv_sample)
```

```python
import functools

import jax
import jax.numpy as jnp
from jax import lax
from jax.experimental import pallas as pl
from jax.experimental.pallas import tpu as pltpu

D_MODEL = 1024
DEPTH = 4
PAGE_SIZE = 128
N_A = DEPTH // 2
D_RNN = 768
N_RG_BLOCKS = 8
RG_BW = D_RNN // N_RG_BLOCKS
CONV_W = 4
RG_C = 8.0
N_HEADS = 12
N_KV = 4
HEAD_DIM = 64
CMP_LEN = 32
CMP_STRIDE = 16
CMP_HID = 128
SEL_BLK = 64
N_SEL = 16
WINDOW = 512
Q_BLK = 64
N_MEM = 256
MEM_HEADS = 4
MEM_HD = 64
MEM_W = MEM_HEADS * MEM_HD
N_EXPERTS = 32
TOP_K = 4
D_EXPERT = 1024
SWIGLU_LIMIT = 7.0
SWIGLU_ALPHA = 1.702
DN_ALPHA = (2 * DEPTH) ** 0.25
LN_EPS = 1e-5

F32 = jnp.float32
BF16 = jnp.bfloat16

ROW_TILE = 512
MOE_BLK = 256
LANES = 128
VMEM_LIMIT = 56 * 1024 * 1024


def _params(*sem):
    return pltpu.CompilerParams(dimension_semantics=sem, vmem_limit_bytes=VMEM_LIMIT)


def _dense_body(x_ref, w_ref, o_ref):
    o_ref[...] = jnp.dot(x_ref[...].astype(BF16), w_ref[...], preferred_element_type=F32)


def dense(x, w_bf16, tm=ROW_TILE):
    M, K = x.shape
    N = w_bf16.shape[1]
    return pl.pallas_call(
        _dense_body,
        grid=(M // tm,),
        in_specs=[pl.BlockSpec((tm, K), lambda i: (i, 0)),
                  pl.BlockSpec((K, N), lambda i: (0, 0))],
        out_specs=pl.BlockSpec((tm, N), lambda i: (i, 0)),
        out_shape=jax.ShapeDtypeStruct((M, N), F32),
        compiler_params=_params("parallel"),
        name="dense",
    )(x, w_bf16)


def _layer_norm_rows(z, g, b):
    mu = jnp.mean(z, axis=-1, keepdims=True)
    zc = z - mu
    var = jnp.mean(zc * zc, axis=-1, keepdims=True)
    return zc * lax.rsqrt(var + LN_EPS) * g + b


def _dense_ln_body(m_ref, w_ref, x_ref, g_ref, b_ref, o_ref):
    y = jnp.dot(m_ref[...].astype(BF16), w_ref[...], preferred_element_type=F32)
    o_ref[...] = _layer_norm_rows(DN_ALPHA * x_ref[...] + y, g_ref[...], b_ref[...])


def dense_post_norm(merged, w_bf16, x, g, b, tm=ROW_TILE):
    M, K = merged.shape
    D = w_bf16.shape[1]
    return pl.pallas_call(
        _dense_ln_body,
        grid=(M // tm,),
        in_specs=[pl.BlockSpec((tm, K), lambda i: (i, 0)),
                  pl.BlockSpec((K, D), lambda i: (0, 0)),
                  pl.BlockSpec((tm, D), lambda i: (i, 0)),
                  pl.BlockSpec((1, D), lambda i: (0, 0)),
                  pl.BlockSpec((1, D), lambda i: (0, 0))],
        out_specs=pl.BlockSpec((tm, D), lambda i: (i, 0)),
        out_shape=jax.ShapeDtypeStruct((M, D), F32),
        compiler_params=_params("parallel"),
        name="dense_post_norm",
    )(merged, w_bf16, x, g.reshape(1, D), b.reshape(1, D))


def _router_body(x_ref, w_ref, b_ref, o_ref):
    o_ref[...] = jnp.dot(x_ref[...], w_ref[...], preferred_element_type=F32,
                         precision=lax.Precision.HIGHEST) + b_ref[...]


def router_logits(x, w_r, b_r, tm=ROW_TILE):
    M, K = x.shape
    w = jnp.pad(w_r, ((0, 0), (0, LANES - N_EXPERTS)))
    b = jnp.pad(b_r, (0, LANES - N_EXPERTS)).reshape(1, LANES)
    out = pl.pallas_call(
        _router_body,
        grid=(M // tm,),
        in_specs=[pl.BlockSpec((tm, K), lambda i: (i, 0)),
                  pl.BlockSpec((K, LANES), lambda i: (0, 0)),
                  pl.BlockSpec((1, LANES), lambda i: (0, 0))],
        out_specs=pl.BlockSpec((tm, LANES), lambda i: (i, 0)),
        out_shape=jax.ShapeDtypeStruct((M, LANES), F32),
        compiler_params=_params("parallel"),
        name="router",
    )(x, w, b)
    return out[:, :N_EXPERTS]


def _experts_body(blk_e_ref, n_used_ref, x_ref, wr_ref, wgu_ref, bgu_ref, wdn_ref, bdn_ref,
                  o_ref, wgu_bf, wdn_bf):
    i = pl.program_id(0)
    e = blk_e_ref[i]
    e_prev = blk_e_ref[jnp.maximum(i - 1, 0)]

    @pl.when((i == 0) | (e != e_prev))
    def _():
        wgu_bf[...] = wgu_ref[0].astype(BF16)
        wdn_bf[...] = wdn_ref[0].astype(BF16)

    @pl.when(i < n_used_ref[0])
    def _():
        h = jnp.dot(x_ref[...].astype(BF16), wgu_bf[...], preferred_element_type=F32) + bgu_ref[0]
        gate = jnp.minimum(h[:, :D_EXPERT], SWIGLU_LIMIT)
        up = jnp.clip(h[:, D_EXPERT:], -SWIGLU_LIMIT, SWIGLU_LIMIT)
        act = (up + 1.0) * gate * jax.nn.sigmoid(SWIGLU_ALPHA * gate)
        y = jnp.dot(act.astype(BF16), wdn_bf[...], preferred_element_type=F32) + bdn_ref[0]
        o_ref[...] = y * wr_ref[...]

    @pl.when(i >= n_used_ref[0])
    def _():
        o_ref[...] = jnp.zeros_like(o_ref)


def experts(x_pad, w_row, blk_e, n_used, w_gu, b_gu, w_dn, b_dn):
    R, D = x_pad.shape
    n_blk = R // MOE_BLK
    grid_spec = pltpu.PrefetchScalarGridSpec(
        num_scalar_prefetch=2,
        grid=(n_blk,),
        in_specs=[
            pl.BlockSpec((MOE_BLK, D), lambda i, be, nu: (i, 0)),
            pl.BlockSpec((MOE_BLK, 1), lambda i, be, nu: (i, 0)),
            pl.BlockSpec((1, D, 2 * D_EXPERT), lambda i, be, nu: (be[i], 0, 0)),
            pl.BlockSpec((1, 1, 2 * D_EXPERT), lambda i, be, nu: (be[i], 0, 0)),
            pl.BlockSpec((1, D_EXPERT, D), lambda i, be, nu: (be[i], 0, 0)),
            pl.BlockSpec((1, 1, D), lambda i, be, nu: (be[i], 0, 0)),
        ],
        out_specs=pl.BlockSpec((MOE_BLK, D), lambda i, be, nu: (i, 0)),
        scratch_shapes=[pltpu.VMEM((D, 2 * D_EXPERT), BF16), pltpu.VMEM((D_EXPERT, D), BF16)],
    )
    return pl.pallas_call(
        _experts_body,
        grid_spec=grid_spec,
        out_shape=jax.ShapeDtypeStruct((R, D), F32),
        compiler_params=_params("arbitrary"),
        name="experts",
    )(blk_e, n_used, x_pad, w_row, w_gu, b_gu.reshape(N_EXPERTS, 1, -1), w_dn,
      b_dn.reshape(N_EXPERTS, 1, -1))


def _combine_ln_body(y_ref, x_ref, g_ref, b_ref, o_ref):
    D = x_ref.shape[1]
    y = y_ref[:, 0:D]
    for k in range(1, TOP_K):
        y = y + y_ref[:, k * D:(k + 1) * D]
    o_ref[...] = _layer_norm_rows(DN_ALPHA * x_ref[...] + y, g_ref[...], b_ref[...])


def combine_post_norm(y_assign, x, g, b, tm=ROW_TILE):
    M, D = x.shape
    return pl.pallas_call(
        _combine_ln_body,
        grid=(M // tm,),
        in_specs=[pl.BlockSpec((tm, TOP_K * D), lambda i: (i, 0)),
                  pl.BlockSpec((tm, D), lambda i: (i, 0)),
                  pl.BlockSpec((1, D), lambda i: (0, 0)),
                  pl.BlockSpec((1, D), lambda i: (0, 0))],
        out_specs=pl.BlockSpec((tm, D), lambda i: (i, 0)),
        out_shape=jax.ShapeDtypeStruct((M, D), F32),
        compiler_params=_params("parallel"),
        name="combine_post_norm",
    )(y_assign, x, g.reshape(1, D), b.reshape(1, D))


def moe_post_norm(x, w_r, b_r, w_gu, b_gu, w_dn, b_dn, g, b):
    N, D = x.shape
    logits = router_logits(x, w_r, b_r)
    top_v, top_e = lax.top_k(logits, TOP_K)
    wts = jax.nn.softmax(top_v, axis=-1)
    NK = N * TOP_K
    flat_e = top_e.reshape(-1)
    order = jnp.argsort(flat_e)
    e_sorted = flat_e[order]
    sizes = jnp.bincount(flat_e, length=N_EXPERTS)
    starts = jnp.cumsum(sizes) - sizes
    padded = (sizes + MOE_BLK - 1) // MOE_BLK * MOE_BLK
    pad_end = jnp.cumsum(padded)
    pad_start = pad_end - padded
    dest = (pad_start[e_sorted] + jnp.arange(NK) - starts[e_sorted]).astype(jnp.int32)
    n_blk = NK // MOE_BLK + N_EXPERTS
    R = n_blk * MOE_BLK
    src_tok = jnp.zeros((R,), jnp.int32).at[dest].set((order // TOP_K).astype(jnp.int32))
    w_row = jnp.zeros((R,), F32).at[dest].set(wts.reshape(-1)[order]).reshape(R, 1)
    pos = jnp.zeros((NK,), jnp.int32).at[order].set(dest)
    blk_e = jnp.minimum(jnp.searchsorted(pad_end, jnp.arange(n_blk) * MOE_BLK, side='right'),
                        N_EXPERTS - 1).astype(jnp.int32)
    n_used = (pad_end[-1] // MOE_BLK).astype(jnp.int32).reshape(1)
    y_pad = experts(x[src_tok], w_row, blk_e, n_used, w_gu, b_gu, w_dn, b_dn)
    return combine_post_norm(y_pad[pos].reshape(N, TOP_K * D), x, g, b)


def masked_softmax(s, mask):
    s = jnp.where(mask, s, -jnp.inf)
    m = jnp.max(s, axis=-1, keepdims=True)
    m = jnp.where(jnp.isfinite(m), m, 0.0)
    p = jnp.exp(s - m)
    d = jnp.sum(p, axis=-1, keepdims=True)
    return p / jnp.where(d > 0, d, 1.0)


def mem_attend(q, mem_kv):
    B, T, _ = q.shape
    qh = q.reshape(B, T, MEM_HEADS, MEM_HD)
    k, v = mem_kv[:, :, 0], mem_kv[:, :, 1]
    s = jnp.einsum('bthd,bmhd->bhtm', qh, k).astype(F32) * (MEM_HD ** -0.5)
    p = jax.nn.softmax(s, axis=-1)
    o = jnp.einsum('bhtm,bmhd->bthd', p.astype(v.dtype), v)
    return o.reshape(B, T, MEM_W)


def _lin_combine(c1, c2):
    a1, b1 = c1
    a2, b2 = c2
    return a1 * a2, a2 * b1 + b2


def rglru_branch(u, conv_buf, h0, conv_w, conv_b, w_a, b_a, w_i, b_i, lam):
    B, T, C = u.shape
    full = jnp.concatenate([conv_buf.astype(u.dtype), u], axis=1)
    xc = lax.conv_general_dilated(full, conv_w.astype(u.dtype)[:, None, :], (1,), 'VALID',
                                  dimension_numbers=('NWC', 'WIO', 'NWC'),
                                  feature_group_count=C) + conv_b
    xb = xc.reshape(B, T, N_RG_BLOCKS, RG_BW)
    r = jax.nn.sigmoid(jnp.einsum('btnc,ncd->btnd', xb, w_a).reshape(B, T, C) + b_a)
    i = jax.nn.sigmoid(jnp.einsum('btnc,ncd->btnd', xb, w_i).reshape(B, T, C) + b_i)
    log_a = -RG_C * r.astype(F32) * jax.nn.softplus(-lam.astype(F32))
    a = jnp.exp(log_a)
    b = jnp.sqrt(-jnp.expm1(2.0 * log_a)) * (i * xc).astype(F32)
    b = b.at[:, 0].add(a[:, 0] * h0.astype(F32))
    _, h = lax.associative_scan(_lin_combine, (a, b), axis=1)
    return h.astype(u.dtype), h[:, -1].astype(u.dtype), full[:, T:]


def compress_tokens(kseq, pe, w1, b1, w2, b2):
    B, T, G, Dh = kseq.shape
    R = CMP_LEN // CMP_STRIDE
    n_chunk = T // CMP_STRIDE
    n_cmp = n_chunk - R + 1
    ch = kseq[:, :n_chunk * CMP_STRIDE].reshape(B, n_chunk, CMP_STRIDE, G, Dh)
    pe_r = pe.reshape(R, CMP_STRIDE, Dh)
    w1_r = w1.reshape(R, CMP_STRIDE, Dh, CMP_HID)
    hid = b1
    for r in range(R):
        hid = hid + jnp.einsum('bnsgd,sdh->bngh', ch[:, r:r + n_cmp] + pe_r[r][:, None, :], w1_r[r])
    out = jnp.einsum('bngh,hd->bngd', jax.nn.gelu(hid), w2) + b2
    ends = jnp.arange(n_cmp) * CMP_STRIDE + (CMP_LEN - 1)
    return out, ends


def nsa_context(kv4, cmp_pe, cmp_w1, cmp_b1, cmp_w2, cmp_b2):
    kc, ends = compress_tokens(kv4[:, :, 0], cmp_pe[0], cmp_w1[0], cmp_b1[0], cmp_w2[0], cmp_b2[0])
    vc, _ = compress_tokens(kv4[:, :, 1], cmp_pe[1], cmp_w1[1], cmp_b1[1], cmp_w2[1], cmp_b2[1])
    B, T = kv4.shape[:2]
    n_sel = -(-T // SEL_BLK)
    sel = jnp.pad(kv4[:, :, 2:4], ((0, 0), (0, n_sel * SEL_BLK - T), (0, 0), (0, 0), (0, 0)))
    sel = sel.reshape(B, n_sel, SEL_BLK, 2, N_KV, HEAD_DIM).transpose(3, 0, 4, 1, 2, 5)
    return kc, vc, ends, sel[0], sel[1]


def nsa_attend(q, gates, q_pos, kc, vc, c_end, ksb, vsb, kw, vw, w_pos):
    B, Qb, H, Dh = q.shape
    G = kc.shape[2]
    R = H // G
    n_sel = ksb.shape[2]
    scale = Dh ** -0.5
    qg = q.reshape(B, Qb, G, R, Dh)
    s_c = jnp.einsum('bqgrd,bngd->bgrqn', qg, kc).astype(F32) * scale
    p_c = masked_softmax(s_c, c_end[None, :] <= q_pos[:, None])
    o_c = jnp.einsum('bgrqn,bngd->bqgrd', p_c.astype(vc.dtype), vc)
    c_start = c_end - (CMP_LEN - 1)
    s_start = jnp.arange(n_sel) * SEL_BLK
    cover = ((c_start[:, None] < s_start[None, :] + SEL_BLK) & (c_end[:, None] >= s_start[None, :])).astype(F32)
    imp = jnp.einsum('bgrqn,nj->bgqj', p_c, cover)
    cur = q_pos // SEL_BLK
    j = jnp.arange(n_sel)
    forced = (j[None, :] == 0) | (j[None, :] == cur[:, None]) | (j[None, :] == cur[:, None] - 1)
    future = j[None, :] > cur[:, None]
    imp = jnp.where(forced, jnp.inf, jnp.where(future, -jnp.inf, imp))
    _, idx = lax.top_k(imp, min(N_SEL, n_sel))
    gather = jax.vmap(jax.vmap(lambda blocks, ids: blocks[ids]))
    ks = gather(ksb, idx)
    vs = gather(vsb, idx)
    n_top = idx.shape[-1]
    k_pos = idx[..., None] * SEL_BLK + jnp.arange(SEL_BLK)
    m_s = (k_pos <= q_pos[:, None, None]).reshape(B, G, 1, Qb, n_top * SEL_BLK)
    s_s = jnp.einsum('bqgrd,bgqksd->bgrqks', qg, ks).astype(F32) * scale
    p_s = masked_softmax(s_s.reshape(B, G, R, Qb, n_top * SEL_BLK), m_s)
    o_s = jnp.einsum('bgrqm,bgqmd->bqgrd', p_s.astype(vs.dtype), vs.reshape(B, G, Qb, n_top * SEL_BLK, Dh))
    dist = q_pos[:, None] - w_pos[None, :]
    m_w = (w_pos[None, :] >= 0) & (dist >= 0) & (dist <= WINDOW)
    s_w = jnp.einsum('bqgrd,bngd->bgrqn', qg, kw).astype(F32) * scale
    p_w = masked_softmax(s_w, m_w)
    o_w = jnp.einsum('bgrqn,bngd->bqgrd', p_w.astype(vw.dtype), vw)
    g = gates.reshape(B, Qb, G, R, 3)
    out = g[..., 0:1] * o_c + g[..., 1:2] * o_s + g[..., 2:3] * o_w
    return out.reshape(B, Qb, H * Dh)


def nsa_prompt(q, gates, ctx, k_win, v_win):
    B, T, H, Dh = q.shape
    n_blk = T // Q_BLK
    zpad = jnp.zeros((B, WINDOW) + k_win.shape[2:], k_win.dtype)
    kw_pad = jnp.concatenate([zpad, k_win], axis=1)
    vw_pad = jnp.concatenate([zpad.astype(v_win.dtype), v_win], axis=1)
    qb = q.reshape(B, n_blk, Q_BLK, H, Dh).swapaxes(0, 1)
    gb = gates.reshape(B, n_blk, Q_BLK, H, 3).swapaxes(0, 1)
    band = jnp.arange(WINDOW + Q_BLK)

    def block(args):
        qi, gi, i = args
        start = i * Q_BLK
        kw = lax.dynamic_slice_in_dim(kw_pad, start, WINDOW + Q_BLK, axis=1)
        vw = lax.dynamic_slice_in_dim(vw_pad, start, WINDOW + Q_BLK, axis=1)
        return nsa_attend(qi, gi, start + jnp.arange(Q_BLK), *ctx, kw, vw, start - WINDOW + band)

    out = lax.map(block, (qb, gb, jnp.arange(n_blk)))
    return out.swapaxes(0, 1).reshape(B, T, H * Dh)


def nsa_sample(q, gates, ctx, k_win, v_win, pos0, w_first):
    B, S, H, Dh = q.shape
    w_pos = w_first + jnp.arange(k_win.shape[1])
    q_pos = pos0 + jnp.arange(S)

    def step(args):
        qi, gi, t = args
        return nsa_attend(qi[:, None], gi[:, None], t[None], *ctx, k_win, v_win, w_pos)

    out = lax.map(step, (q.swapaxes(0, 1), gates.swapaxes(0, 1), q_pos))
    return out[:, :, 0].swapaxes(0, 1)


def kernel(x_prompt, x_sample, cache_kv, cache_win, cache_mem_kv, state_rg_h, state_rg_conv,
           page_table, mem_prompt, ln_g, ln_b, w_in_a, conv_w, conv_b, w_rg_a, b_rg_a,
           w_rg_i, b_rg_i, rg_lambda, w_out_a, w_kv_shared, cmp_pe, cmp_w1, cmp_b1, cmp_w2,
           cmp_b2, w_in_b, w_out_b, w_mem_kv, w_router, b_router, w_gate_up, b_gate_up,
           w_down, b_down):
    hq = N_HEADS * HEAD_DIM
    Bp, Tp, D = x_prompt.shape
    Bs, Ss, _ = x_sample.shape
    Np, Ns = Bp * Tp, Bs * Ss
    past_len = page_table.shape[1] * PAGE_SIZE
    w_buf = cache_win.shape[1]
    cmp_params = (cmp_pe, cmp_w1, cmp_b1, cmp_w2, cmp_b2)

    def split(t):
        return t[:Np].reshape(Bp, Tp, -1), t[Np:].reshape(Bs, Ss, -1)

    def join(tp, ts):
        return jnp.concatenate([tp.reshape(Np, -1), ts.reshape(Ns, -1)], axis=0)

    w_mem_all = w_mem_kv.transpose(1, 0, 2).reshape(D, DEPTH * 2 * MEM_W).astype(BF16)
    mem_all = dense(mem_prompt.reshape(Bp * N_MEM, D), w_mem_all, tm=Bp * N_MEM // 2)
    mem_kv_prompt = mem_all.reshape(Bp, N_MEM, DEPTH, 2, MEM_HEADS, MEM_HD).transpose(2, 0, 1, 3, 4, 5)

    x = join(x_prompt, x_sample)
    hs_p, hs_s, bufs_p, bufs_s = [], [], [], []
    for l in range(DEPTH):
        if l < N_A:
            p = dense(x, w_in_a[l].astype(BF16))
            pp, ps = split(p)
            mixed = []
            for grp, pg, mem_kv, conv_buf, h0 in (
                    (0, pp, mem_kv_prompt[l], jnp.zeros((Bp, CONV_W - 1, D_RNN), F32),
                     jnp.zeros((Bp, D_RNN), F32)),
                    (1, ps, cache_mem_kv[l], state_rg_conv[l], state_rg_h[l])):
                gate = jax.nn.gelu(pg[..., :D_RNN])
                u = pg[..., D_RNN:2 * D_RNN]
                q_mem = pg[..., 2 * D_RNN:]
                h, h_last, buf = rglru_branch(u, conv_buf, h0, conv_w[l], conv_b[l], w_rg_a[l],
                                              b_rg_a[l], w_rg_i[l], b_rg_i[l], rg_lambda[l])
                mixed.append(jnp.concatenate([h * gate, mem_attend(q_mem, mem_kv)], axis=-1))
                (hs_p, hs_s)[grp].append(h_last)
                (bufs_p, bufs_s)[grp].append(buf)
            x = dense_post_norm(join(*mixed), w_out_a[l].astype(BF16), x, ln_g[l, 0], ln_b[l, 0])
        else:
            jb = l - N_A
            if l == N_A:
                kv = dense(x, w_kv_shared.astype(BF16))
                kv_p, kv_s = split(kv)
                kv_p = kv_p.reshape(Bp, Tp, 6, N_KV, HEAD_DIM)
                kv_s = kv_s.reshape(Bs, Ss, 6, N_KV, HEAD_DIM)
                ctx_p = nsa_context(kv_p[:, :, :4], *cmp_params)
                past = cache_kv[page_table].reshape(Bs, past_len, 4, N_KV, HEAD_DIM)
                ctx_s = nsa_context(jnp.concatenate([past, kv_s[:, :, :4]], axis=1), *cmp_params)
                win_s = jnp.concatenate([cache_win, kv_s[:, :, 4:]], axis=1)
            w_in = jnp.pad(w_in_b[jb], ((0, 0), (0, 1152 - w_in_b.shape[2]))).astype(BF16)
            p = dense(x, w_in)
            pp, ps = split(p)
            mixed = []
            for grp, pg, mem_kv in ((0, pp, mem_kv_prompt[l]), (1, ps, cache_mem_kv[l])):
                B, T = pg.shape[:2]
                q = pg[..., :hq].reshape(B, T, N_HEADS, HEAD_DIM)
                gates = jax.nn.sigmoid(pg[..., hq:hq + 3 * N_HEADS]).reshape(B, T, N_HEADS, 3)
                q_mem = pg[..., hq + 3 * N_HEADS:hq + 3 * N_HEADS + MEM_W]
                if grp == 0:
                    o = nsa_prompt(q, gates, ctx_p, kv_p[:, :, 4], kv_p[:, :, 5])
                else:
                    o = nsa_sample(q, gates, ctx_s, win_s[:, :, 0], win_s[:, :, 1],
                                   past_len, past_len - w_buf)
                mixed.append(jnp.concatenate([o, mem_attend(q_mem, mem_kv)], axis=-1))
            x = dense_post_norm(join(*mixed), w_out_b[jb].astype(BF16), x, ln_g[l, 0], ln_b[l, 0])
        x = moe_post_norm(x, w_router[l], b_router[l], w_gate_up[l], b_gate_up[l], w_down[l],
                          b_down[l], ln_g[l, 1], ln_b[l, 1])

    y_prompt, y_sample = split(x)
    kv_rows_prompt = kv_p[:, :, :4]
    kv_rows_sample = kv_s[:, :, :4]
    win_prompt = kv_p[:, Tp - min(WINDOW, Tp):, 4:]
    win_sample = win_s[:, win_s.shape[1] - min(WINDOW, past_len + Ss):]
    return (y_prompt, y_sample, kv_rows_prompt, kv_rows_sample, win_prompt, win_sample,
            mem_kv_prompt, jnp.stack(hs_p), jnp.stack(hs_s), jnp.stack(bufs_p), jnp.stack(bufs_s))
```

```python
import functools

import jax
import jax.numpy as jnp
from jax import lax
from jax.experimental import pallas as pl
from jax.experimental.pallas import tpu as pltpu

D_MODEL = 1024
DEPTH = 4
PAGE_SIZE = 128
N_A = DEPTH // 2
D_RNN = 768
N_RG_BLOCKS = 8
RG_BW = D_RNN // N_RG_BLOCKS
CONV_W = 4
RG_C = 8.0
N_HEADS = 12
N_KV = 4
HEAD_DIM = 64
CMP_LEN = 32
CMP_STRIDE = 16
CMP_HID = 128
SEL_BLK = 64
N_SEL = 16
WINDOW = 512
Q_BLK = 64
N_MEM = 256
MEM_HEADS = 4
MEM_HD = 64
MEM_W = MEM_HEADS * MEM_HD
N_EXPERTS = 32
TOP_K = 4
D_EXPERT = 1024
SWIGLU_LIMIT = 7.0
SWIGLU_ALPHA = 1.702
DN_ALPHA = (2 * DEPTH) ** 0.25
LN_EPS = 1e-5

F32 = jnp.float32
BF16 = jnp.bfloat16

ROW_TILE = 512
MOE_BLK = 256
LANES = 128
VMEM_LIMIT = 56 * 1024 * 1024


def _params(*sem):
    return pltpu.CompilerParams(dimension_semantics=sem, vmem_limit_bytes=VMEM_LIMIT)


def _dense_body(x_ref, w_ref, o_ref):
    o_ref[...] = jnp.dot(x_ref[...].astype(BF16), w_ref[...], preferred_element_type=F32)


def dense(x, w_bf16, tm=ROW_TILE):
    M, K = x.shape
    N = w_bf16.shape[1]
    return pl.pallas_call(
        _dense_body,
        grid=(M // tm,),
        in_specs=[pl.BlockSpec((tm, K), lambda i: (i, 0)),
                  pl.BlockSpec((K, N), lambda i: (0, 0))],
        out_specs=pl.BlockSpec((tm, N), lambda i: (i, 0)),
        out_shape=jax.ShapeDtypeStruct((M, N), F32),
        compiler_params=_params("parallel"),
        name="dense",
    )(x, w_bf16)


def _layer_norm_rows(z, g, b):
    mu = jnp.mean(z, axis=-1, keepdims=True)
    zc = z - mu
    var = jnp.mean(zc * zc, axis=-1, keepdims=True)
    return zc * lax.rsqrt(var + LN_EPS) * g + b


def _dense_ln_body(m_ref, w_ref, x_ref, g_ref, b_ref, o_ref):
    y = jnp.dot(m_ref[...].astype(BF16), w_ref[...], preferred_element_type=F32)
    o_ref[...] = _layer_norm_rows(DN_ALPHA * x_ref[...] + y, g_ref[...], b_ref[...])


def dense_post_norm(merged, w_bf16, x, g, b, tm=ROW_TILE):
    M, K = merged.shape
    D = w_bf16.shape[1]
    return pl.pallas_call(
        _dense_ln_body,
        grid=(M // tm,),
        in_specs=[pl.BlockSpec((tm, K), lambda i: (i, 0)),
                  pl.BlockSpec((K, D), lambda i: (0, 0)),
                  pl.BlockSpec((tm, D), lambda i: (i, 0)),
                  pl.BlockSpec((1, D), lambda i: (0, 0)),
                  pl.BlockSpec((1, D), lambda i: (0, 0))],
        out_specs=pl.BlockSpec((tm, D), lambda i: (i, 0)),
        out_shape=jax.ShapeDtypeStruct((M, D), F32),
        compiler_params=_params("parallel"),
        name="dense_post_norm",
    )(merged, w_bf16, x, g.reshape(1, D), b.reshape(1, D))


def _dense_ln_b_body(a_ref, m_ref, wa_ref, wm_ref, x_ref, g_ref, b_ref, o_ref):
    y = jnp.dot(m_ref[...].astype(BF16), wm_ref[...], preferred_element_type=F32)
    for grp in range(N_KV):
        y = y + jnp.dot(a_ref[grp].astype(BF16), wa_ref[grp], preferred_element_type=F32)
    o_ref[...] = _layer_norm_rows(DN_ALPHA * x_ref[...] + y, g_ref[...], b_ref[...])


def dense_post_norm_b(o_nsa, o_mem, w_out, x, g, b, tm=ROW_TILE):
    G, M, S = o_nsa.shape
    D = w_out.shape[1]
    hq = N_HEADS * HEAD_DIM
    wa = w_out[:hq].reshape(G, hq // G, D)
    wa = jnp.pad(wa, ((0, 0), (0, S - hq // G), (0, 0))).astype(BF16)
    wm = w_out[hq:].astype(BF16)
    return pl.pallas_call(
        _dense_ln_b_body,
        grid=(M // tm,),
        in_specs=[pl.BlockSpec((G, tm, S), lambda i: (0, i, 0)),
                  pl.BlockSpec((tm, MEM_W), lambda i: (i, 0)),
                  pl.BlockSpec((G, S, D), lambda i: (0, 0, 0)),
                  pl.BlockSpec((MEM_W, D), lambda i: (0, 0)),
                  pl.BlockSpec((tm, D), lambda i: (i, 0)),
                  pl.BlockSpec((1, D), lambda i: (0, 0)),
                  pl.BlockSpec((1, D), lambda i: (0, 0))],
        out_specs=pl.BlockSpec((tm, D), lambda i: (i, 0)),
        out_shape=jax.ShapeDtypeStruct((M, D), F32),
        compiler_params=_params("parallel"),
        name="dense_post_norm_b",
    )(o_nsa, o_mem, wa, wm, x, g.reshape(1, D), b.reshape(1, D))


def _router_body(x_ref, w_ref, b_ref, o_ref):
    o_ref[...] = jnp.dot(x_ref[...], w_ref[...], preferred_element_type=F32,
                         precision=lax.Precision.HIGHEST) + b_ref[...]


def router_logits(x, w_r, b_r, tm=ROW_TILE):
    M, K = x.shape
    w = jnp.pad(w_r, ((0, 0), (0, LANES - N_EXPERTS)))
    b = jnp.pad(b_r, (0, LANES - N_EXPERTS)).reshape(1, LANES)
    out = pl.pallas_call(
        _router_body,
        grid=(M // tm,),
        in_specs=[pl.BlockSpec((tm, K), lambda i: (i, 0)),
                  pl.BlockSpec((K, LANES), lambda i: (0, 0)),
                  pl.BlockSpec((1, LANES), lambda i: (0, 0))],
        out_specs=pl.BlockSpec((tm, LANES), lambda i: (i, 0)),
        out_shape=jax.ShapeDtypeStruct((M, LANES), F32),
        compiler_params=_params("parallel"),
        name="router",
    )(x, w, b)
    return out[:, :N_EXPERTS]


def _experts_body(blk_e_ref, n_used_ref, x_ref, wr_ref, wgu_ref, bgu_ref, wdn_ref, bdn_ref,
                  o_ref, wgu_bf, wdn_bf):
    i = pl.program_id(0)
    e = blk_e_ref[i]
    e_prev = blk_e_ref[jnp.maximum(i - 1, 0)]

    @pl.when((i == 0) | (e != e_prev))
    def _():
        wgu_bf[...] = wgu_ref[0].astype(BF16)
        wdn_bf[...] = wdn_ref[0].astype(BF16)

    @pl.when(i < n_used_ref[0])
    def _():
        h = jnp.dot(x_ref[...].astype(BF16), wgu_bf[...], preferred_element_type=F32) + bgu_ref[0]
        gate = jnp.minimum(h[:, :D_EXPERT], SWIGLU_LIMIT)
        up = jnp.clip(h[:, D_EXPERT:], -SWIGLU_LIMIT, SWIGLU_LIMIT)
        act = (up + 1.0) * gate * jax.nn.sigmoid(SWIGLU_ALPHA * gate)
        y = jnp.dot(act.astype(BF16), wdn_bf[...], preferred_element_type=F32) + bdn_ref[0]
        o_ref[...] = y * wr_ref[...]

    @pl.when(i >= n_used_ref[0])
    def _():
        o_ref[...] = jnp.zeros_like(o_ref)


def experts(x_pad, w_row, blk_e, n_used, w_gu, b_gu, w_dn, b_dn):
    R, D = x_pad.shape
    n_blk = R // MOE_BLK
    grid_spec = pltpu.PrefetchScalarGridSpec(
        num_scalar_prefetch=2,
        grid=(n_blk,),
        in_specs=[
            pl.BlockSpec((MOE_BLK, D), lambda i, be, nu: (i, 0)),
            pl.BlockSpec((MOE_BLK, 1), lambda i, be, nu: (i, 0)),
            pl.BlockSpec((1, D, 2 * D_EXPERT), lambda i, be, nu: (be[i], 0, 0)),
            pl.BlockSpec((1, 1, 2 * D_EXPERT), lambda i, be, nu: (be[i], 0, 0)),
            pl.BlockSpec((1, D_EXPERT, D), lambda i, be, nu: (be[i], 0, 0)),
            pl.BlockSpec((1, 1, D), lambda i, be, nu: (be[i], 0, 0)),
        ],
        out_specs=pl.BlockSpec((MOE_BLK, D), lambda i, be, nu: (i, 0)),
        scratch_shapes=[pltpu.VMEM((D, 2 * D_EXPERT), BF16), pltpu.VMEM((D_EXPERT, D), BF16)],
    )
    return pl.pallas_call(
        _experts_body,
        grid_spec=grid_spec,
        out_shape=jax.ShapeDtypeStruct((R, D), F32),
        compiler_params=_params("arbitrary"),
        name="experts",
    )(blk_e, n_used, x_pad, w_row, w_gu, b_gu.reshape(N_EXPERTS, 1, -1), w_dn,
      b_dn.reshape(N_EXPERTS, 1, -1))


def _combine_ln_body(y_ref, x_ref, g_ref, b_ref, o_ref):
    D = x_ref.shape[1]
    y = y_ref[:, 0:D]
    for k in range(1, TOP_K):
        y = y + y_ref[:, k * D:(k + 1) * D]
    o_ref[...] = _layer_norm_rows(DN_ALPHA * x_ref[...] + y, g_ref[...], b_ref[...])


def combine_post_norm(y_assign, x, g, b, tm=ROW_TILE):
    M, D = x.shape
    return pl.pallas_call(
        _combine_ln_body,
        grid=(M // tm,),
        in_specs=[pl.BlockSpec((tm, TOP_K * D), lambda i: (i, 0)),
                  pl.BlockSpec((tm, D), lambda i: (i, 0)),
                  pl.BlockSpec((1, D), lambda i: (0, 0)),
                  pl.BlockSpec((1, D), lambda i: (0, 0))],
        out_specs=pl.BlockSpec((tm, D), lambda i: (i, 0)),
        out_shape=jax.ShapeDtypeStruct((M, D), F32),
        compiler_params=_params("parallel"),
        name="combine_post_norm",
    )(y_assign, x, g.reshape(1, D), b.reshape(1, D))


def moe_post_norm(x, w_r, b_r, w_gu, b_gu, w_dn, b_dn, g, b):
    N, D = x.shape
    logits = router_logits(x, w_r, b_r)
    top_v, top_e = lax.top_k(logits, TOP_K)
    wts = jax.nn.softmax(top_v, axis=-1)
    NK = N * TOP_K
    flat_e = top_e.reshape(-1)
    order = jnp.argsort(flat_e)
    e_sorted = flat_e[order]
    sizes = jnp.bincount(flat_e, length=N_EXPERTS)
    starts = jnp.cumsum(sizes) - sizes
    padded = (sizes + MOE_BLK - 1) // MOE_BLK * MOE_BLK
    pad_end = jnp.cumsum(padded)
    pad_start = pad_end - padded
    dest = (pad_start[e_sorted] + jnp.arange(NK) - starts[e_sorted]).astype(jnp.int32)
    n_blk = NK // MOE_BLK + N_EXPERTS
    R = n_blk * MOE_BLK
    src_tok = jnp.zeros((R,), jnp.int32).at[dest].set((order // TOP_K).astype(jnp.int32))
    w_row = jnp.zeros((R,), F32).at[dest].set(wts.reshape(-1)[order]).reshape(R, 1)
    pos = jnp.zeros((NK,), jnp.int32).at[order].set(dest)
    blk_e = jnp.minimum(jnp.searchsorted(pad_end, jnp.arange(n_blk) * MOE_BLK, side='right'),
                        N_EXPERTS - 1).astype(jnp.int32)
    n_used = (pad_end[-1] // MOE_BLK).astype(jnp.int32).reshape(1)
    y_pad = experts(x[src_tok], w_row, blk_e, n_used, w_gu, b_gu, w_dn, b_dn)
    return combine_post_norm(y_pad[pos].reshape(N, TOP_K * D), x, g, b)


NSA_TQ = 128
NSA_TK = 512
NSA_TW = 128
HEADS_PER_KV = N_HEADS // N_KV
MASK_OFF = -(2.0 ** 30)
SLAB = 2 * LANES


def _flash_step(s, v, m_ref, l_ref, acc_ref):
    m_old = m_ref[...]
    m_new = jnp.maximum(m_old, jnp.max(s, axis=1, keepdims=True))
    alpha = jnp.exp(m_old - m_new)
    p = jnp.exp(s - m_new)
    l_ref[...] = alpha * l_ref[...] + jnp.sum(p, axis=1, keepdims=True)
    acc_ref[...] = alpha * acc_ref[...] + jnp.dot(p.astype(BF16), v, preferred_element_type=F32)
    m_ref[...] = m_new


def _nt_dot(a, b):
    return lax.dot_general(a, b, (((1,), (1,)), ((), ())), preferred_element_type=F32)


def _nsa_prompt_body(p_ref, ks_ref, kw_ref, kc_ref, covt_ref, o_ref,
                     qaug_ref, m_ref, l_ref, acc_ref, *, n_cmp, n_sel):
    TQ, R = NSA_TQ, HEADS_PER_KV
    i = pl.program_id(2)
    q0 = i * TQ
    slab = p_ref[...]
    a, bv = slab[:, :LANES], slab[:, LANES:]
    lane = lax.broadcasted_iota(jnp.int32, (TQ, LANES), 1)
    lo = lane < HEAD_DIM
    scale = HEAD_DIM ** -0.5
    q_heads = (jnp.where(lo, a, 0.0), jnp.where(lo, pltpu.roll(a, HEAD_DIM, 1), 0.0),
               jnp.where(lo, bv, 0.0))
    q3 = (jnp.concatenate(q_heads, axis=0) * scale).astype(BF16)
    row_q = q0 + lax.broadcasted_iota(jnp.int32, (TQ, 1), 0)
    row_q3 = jnp.concatenate([row_q] * R, axis=0)

    kc = kc_ref[0, 0]
    nc = kc.shape[0]
    s = _nt_dot(q3, kc)
    n_idx = lax.broadcasted_iota(jnp.int32, (R * TQ, nc), 1)
    vis = (n_idx * CMP_STRIDE + (CMP_LEN - 1) <= row_q3) & (n_idx < n_cmp)
    s = jnp.where(vis, s, MASK_OFF)
    m = jnp.max(s, axis=1, keepdims=True)
    p = jnp.where(vis, jnp.exp(s - m), 0.0)
    d = jnp.sum(p, axis=1, keepdims=True)
    p_c = p / jnp.where(d > 0, d, 1.0)
    o_c = jnp.dot(p_c.astype(BF16), kc, preferred_element_type=F32)

    p_sum = p_c[0:TQ]
    for r in range(1, R):
        p_sum = p_sum + p_c[r * TQ:(r + 1) * TQ]
    p_hi = p_sum.astype(BF16)
    p_lo = (p_sum - p_hi.astype(F32)).astype(BF16)
    covt = covt_ref[...]
    imp = _nt_dot(covt, p_hi) + _nt_dot(covt, p_lo)
    j = lax.broadcasted_iota(jnp.int32, (n_sel, TQ), 0)
    cur = jnp.right_shift(q0 + lax.broadcasted_iota(jnp.int32, (n_sel, TQ), 1),
                          SEL_BLK.bit_length() - 1)
    forced = (j == 0) | (j == cur) | (j == cur - 1)
    imp = jnp.where(forced, jnp.inf, jnp.where(j > cur, -jnp.inf, imp))
    rank = jnp.zeros((n_sel, TQ), F32)
    for b in range(n_sel):
        row = imp[b:b + 1, :]
        ahead = (row > imp) | ((row == imp) & (j > b))
        rank = rank + jnp.where(ahead, 1.0, 0.0)
    sel_off = jnp.where(rank < float(min(N_SEL, n_sel)), 0.0, MASK_OFF)
    sel_off = jnp.concatenate([sel_off, jnp.zeros((LANES - n_sel, TQ), F32)], axis=0)
    mask_q = sel_off.T.astype(BF16)
    qaug_ref[:, :LANES] = q3
    qaug_ref[:, LANES:] = jnp.concatenate([mask_q] * R, axis=0)

    def reset():
        m_ref[...] = jnp.full(m_ref.shape, MASK_OFF, F32)
        l_ref[...] = jnp.zeros(l_ref.shape, F32)
        acc_ref[...] = jnp.zeros(acc_ref.shape, F32)

    def finish():
        return acc_ref[...] / l_ref[...]

    reset()
    n_full = q0 // NSA_TK

    def sel_tile(t, masked):
        k0 = pl.multiple_of(t * NSA_TK, NSA_TK)
        kt = ks_ref[0, 0, pl.ds(k0, NSA_TK), :]
        s = _nt_dot(qaug_ref[...], kt)
        if masked:
            k_pos = k0 + lax.broadcasted_iota(jnp.int32, (R * TQ, NSA_TK), 1)
            s = jnp.where(k_pos <= row_q3, s, MASK_OFF)
        _flash_step(s, kt[:, :LANES], m_ref, l_ref, acc_ref)

    def sel_loop(t, c):
        sel_tile(t, False)
        return c

    lax.fori_loop(0, n_full, sel_loop, 0)
    sel_tile(n_full, True)
    o_s = finish()

    reset()
    w0 = jnp.maximum(q0 - WINDOW, 0)
    n_w = (q0 + TQ - w0) // NSA_TW

    def win_loop(u, c):
        k0 = pl.multiple_of(w0 + u * NSA_TW, NSA_TW)
        kt = kw_ref[0, 0, pl.ds(k0, NSA_TW), :]
        s = _nt_dot(q3, kt)
        dist = row_q3 - (k0 + lax.broadcasted_iota(jnp.int32, (R * TQ, NSA_TW), 1))
        s = jnp.where((dist >= 0) & (dist <= WINDOW), s, MASK_OFF)
        _flash_step(s, kt, m_ref, l_ref, acc_ref)
        return c

    lax.fori_loop(0, n_w, win_loop, 0)
    o_w = finish()

    gates = jax.nn.sigmoid(bv)
    outs = []
    for r in range(R):
        c0 = HEAD_DIM + 3 * r
        rows = slice(r * TQ, (r + 1) * TQ)
        outs.append(gates[:, c0:c0 + 1] * o_c[rows] + gates[:, c0 + 1:c0 + 2] * o_s[rows]
                    + gates[:, c0 + 2:c0 + 3] * o_w[rows])
    o_ref[0, :, :LANES] = jnp.where(lo, pltpu.roll(outs[0], HEAD_DIM, 1), outs[1])
    o_ref[0, :, LANES:] = jnp.where(lo, pltpu.roll(outs[2], HEAD_DIM, 1), 0.0)


def nsa_prompt_pallas(p, ks, kw, kc, covt, B, T, n_cmp):
    n_sel = covt.shape[0]
    nq = T // NSA_TQ
    rows = HEADS_PER_KV * NSA_TQ
    body = functools.partial(_nsa_prompt_body, n_cmp=n_cmp, n_sel=n_sel)
    return pl.pallas_call(
        body,
        grid=(B, N_KV, nq),
        in_specs=[pl.BlockSpec((NSA_TQ, SLAB), lambda b, g, i: (b * nq + i, g)),
                  pl.BlockSpec((1, 1, T, SLAB), lambda b, g, i: (b, g, 0, 0)),
                  pl.BlockSpec((1, 1, T, LANES), lambda b, g, i: (b, g, 0, 0)),
                  pl.BlockSpec((1, 1) + kc.shape[2:], lambda b, g, i: (b, g, 0, 0)),
                  pl.BlockSpec(covt.shape, lambda b, g, i: (0, 0))],
        out_specs=pl.BlockSpec((1, NSA_TQ, SLAB), lambda b, g, i: (g, b * nq + i, 0)),
        out_shape=jax.ShapeDtypeStruct((N_KV, B * T, SLAB), F32),
        scratch_shapes=[pltpu.VMEM((rows, SLAB), BF16), pltpu.VMEM((rows, 1), F32),
                        pltpu.VMEM((rows, 1), F32), pltpu.VMEM((rows, LANES), F32)],
        compiler_params=_params("parallel", "parallel", "arbitrary"),
        name="nsa_prompt",
    )(p, ks, kw, kc, covt)


def nsa_prompt_operands(kv_p, kc, vc):
    B, T = kv_p.shape[:2]
    n_sel = T // SEL_BLK
    n_chunk = T // CMP_STRIDE
    n_cmp = kc.shape[1]
    kvt = kv_p.transpose(2, 0, 3, 1, 4).astype(BF16)
    e_t = (jnp.arange(T)[:, None] // SEL_BLK == jnp.arange(LANES)[None, :]).astype(BF16)
    e_t = jnp.broadcast_to(e_t, (B, N_KV, T, LANES))
    ks = jnp.concatenate([kvt[2], kvt[3], e_t], axis=-1)
    kw = jnp.concatenate([kvt[4], kvt[5]], axis=-1)
    kcv = jnp.concatenate([kc, vc], axis=-1).transpose(0, 2, 1, 3)
    kcv = jnp.pad(kcv, ((0, 0), (0, 0), (0, n_chunk - n_cmp), (0, 0))).astype(BF16)
    c_start = jnp.arange(n_chunk) * CMP_STRIDE
    s_start = jnp.arange(n_sel) * SEL_BLK
    covt = ((c_start[None, :] < s_start[:, None] + SEL_BLK)
            & (c_start[None, :] + CMP_LEN - 1 >= s_start[:, None])
            & (jnp.arange(n_chunk)[None, :] < n_cmp)).astype(BF16)
    return ks, kw, kcv, covt


def masked_softmax(s, mask):
    s = jnp.where(mask, s, -jnp.inf)
    m = jnp.max(s, axis=-1, keepdims=True)
    m = jnp.where(jnp.isfinite(m), m, 0.0)
    p = jnp.exp(s - m)
    d = jnp.sum(p, axis=-1, keepdims=True)
    return p / jnp.where(d > 0, d, 1.0)


def mem_attend(q, mem_kv):
    B, T, _ = q.shape
    qh = q.reshape(B, T, MEM_HEADS, MEM_HD)
    k, v = mem_kv[:, :, 0], mem_kv[:, :, 1]
    s = jnp.einsum('bthd,bmhd->bhtm', qh, k).astype(F32) * (MEM_HD ** -0.5)
    p = jax.nn.softmax(s, axis=-1)
    o = jnp.einsum('bhtm,bmhd->bthd', p.astype(v.dtype), v)
    return o.reshape(B, T, MEM_W)


def _lin_combine(c1, c2):
    a1, b1 = c1
    a2, b2 = c2
    return a1 * a2, a2 * b1 + b2


def rglru_branch(u, conv_buf, h0, conv_w, conv_b, w_a, b_a, w_i, b_i, lam):
    B, T, C = u.shape
    full = jnp.concatenate([conv_buf.astype(u.dtype), u], axis=1)
    xc = lax.conv_general_dilated(full, conv_w.astype(u.dtype)[:, None, :], (1,), 'VALID',
                                  dimension_numbers=('NWC', 'WIO', 'NWC'),
                                  feature_group_count=C) + conv_b
    xb = xc.reshape(B, T, N_RG_BLOCKS, RG_BW)
    r = jax.nn.sigmoid(jnp.einsum('btnc,ncd->btnd', xb, w_a).reshape(B, T, C) + b_a)
    i = jax.nn.sigmoid(jnp.einsum('btnc,ncd->btnd', xb, w_i).reshape(B, T, C) + b_i)
    log_a = -RG_C * r.astype(F32) * jax.nn.softplus(-lam.astype(F32))
    a = jnp.exp(log_a)
    b = jnp.sqrt(-jnp.expm1(2.0 * log_a)) * (i * xc).astype(F32)
    b = b.at[:, 0].add(a[:, 0] * h0.astype(F32))
    _, h = lax.associative_scan(_lin_combine, (a, b), axis=1)
    return h.astype(u.dtype), h[:, -1].astype(u.dtype), full[:, T:]


def compress_tokens(kseq, pe, w1, b1, w2, b2):
    B, T, G, Dh = kseq.shape
    R = CMP_LEN // CMP_STRIDE
    n_chunk = T // CMP_STRIDE
    n_cmp = n_chunk - R + 1
    ch = kseq[:, :n_chunk * CMP_STRIDE].reshape(B, n_chunk, CMP_STRIDE, G, Dh)
    pe_r = pe.reshape(R, CMP_STRIDE, Dh)
    w1_r = w1.reshape(R, CMP_STRIDE, Dh, CMP_HID)
    hid = b1
    for r in range(R):
        hid = hid + jnp.einsum('bnsgd,sdh->bngh', ch[:, r:r + n_cmp] + pe_r[r][:, None, :], w1_r[r])
    out = jnp.einsum('bngh,hd->bngd', jax.nn.gelu(hid), w2) + b2
    ends = jnp.arange(n_cmp) * CMP_STRIDE + (CMP_LEN - 1)
    return out, ends


def nsa_context(kv4, cmp_pe, cmp_w1, cmp_b1, cmp_w2, cmp_b2):
    kc, ends = compress_tokens(kv4[:, :, 0], cmp_pe[0], cmp_w1[0], cmp_b1[0], cmp_w2[0], cmp_b2[0])
    vc, _ = compress_tokens(kv4[:, :, 1], cmp_pe[1], cmp_w1[1], cmp_b1[1], cmp_w2[1], cmp_b2[1])
    B, T = kv4.shape[:2]
    n_sel = -(-T // SEL_BLK)
    sel = jnp.pad(kv4[:, :, 2:4], ((0, 0), (0, n_sel * SEL_BLK - T), (0, 0), (0, 0), (0, 0)))
    sel = sel.reshape(B, n_sel, SEL_BLK, 2, N_KV, HEAD_DIM).transpose(3, 0, 4, 1, 2, 5)
    return kc, vc, ends, sel[0], sel[1]


def nsa_attend(q, gates, q_pos, kc, vc, c_end, ksb, vsb, kw, vw, w_pos):
    B, Qb, H, Dh = q.shape
    G = kc.shape[2]
    R = H // G
    n_sel = ksb.shape[2]
    scale = Dh ** -0.5
    qg = q.reshape(B, Qb, G, R, Dh)
    s_c = jnp.einsum('bqgrd,bngd->bgrqn', qg, kc).astype(F32) * scale
    p_c = masked_softmax(s_c, c_end[None, :] <= q_pos[:, None])
    o_c = jnp.einsum('bgrqn,bngd->bqgrd', p_c.astype(vc.dtype), vc)
    c_start = c_end - (CMP_LEN - 1)
    s_start = jnp.arange(n_sel) * SEL_BLK
    cover = ((c_start[:, None] < s_start[None, :] + SEL_BLK) & (c_end[:, None] >= s_start[None, :])).astype(F32)
    imp = jnp.einsum('bgrqn,nj->bgqj', p_c, cover)
    cur = q_pos // SEL_BLK
    j = jnp.arange(n_sel)
    forced = (j[None, :] == 0) | (j[None, :] == cur[:, None]) | (j[None, :] == cur[:, None] - 1)
    future = j[None, :] > cur[:, None]
    imp = jnp.where(forced, jnp.inf, jnp.where(future, -jnp.inf, imp))
    _, idx = lax.top_k(imp, min(N_SEL, n_sel))
    gather = jax.vmap(jax.vmap(lambda blocks, ids: blocks[ids]))
    ks = gather(ksb, idx)
    vs = gather(vsb, idx)
    n_top = idx.shape[-1]
    k_pos = idx[..., None] * SEL_BLK + jnp.arange(SEL_BLK)
    m_s = (k_pos <= q_pos[:, None, None]).reshape(B, G, 1, Qb, n_top * SEL_BLK)
    s_s = jnp.einsum('bqgrd,bgqksd->bgrqks', qg, ks).astype(F32) * scale
    p_s = masked_softmax(s_s.reshape(B, G, R, Qb, n_top * SEL_BLK), m_s)
    o_s = jnp.einsum('bgrqm,bgqmd->bqgrd', p_s.astype(vs.dtype), vs.reshape(B, G, Qb, n_top * SEL_BLK, Dh))
    dist = q_pos[:, None] - w_pos[None, :]
    m_w = (w_pos[None, :] >= 0) & (dist >= 0) & (dist <= WINDOW)
    s_w = jnp.einsum('bqgrd,bngd->bgrqn', qg, kw).astype(F32) * scale
    p_w = masked_softmax(s_w, m_w)
    o_w = jnp.einsum('bgrqn,bngd->bqgrd', p_w.astype(vw.dtype), vw)
    g = gates.reshape(B, Qb, G, R, 3)
    out = g[..., 0:1] * o_c + g[..., 1:2] * o_s + g[..., 2:3] * o_w
    return out.reshape(B, Qb, H * Dh)


def nsa_prompt(q, gates, ctx, k_win, v_win):
    B, T, H, Dh = q.shape
    n_blk = T // Q_BLK
    zpad = jnp.zeros((B, WINDOW) + k_win.shape[2:], k_win.dtype)
    kw_pad = jnp.concatenate([zpad, k_win], axis=1)
    vw_pad = jnp.concatenate([zpad.astype(v_win.dtype), v_win], axis=1)
    qb = q.reshape(B, n_blk, Q_BLK, H, Dh).swapaxes(0, 1)
    gb = gates.reshape(B, n_blk, Q_BLK, H, 3).swapaxes(0, 1)
    band = jnp.arange(WINDOW + Q_BLK)

    def block(args):
        qi, gi, i = args
        start = i * Q_BLK
        kw = lax.dynamic_slice_in_dim(kw_pad, start, WINDOW + Q_BLK, axis=1)
        vw = lax.dynamic_slice_in_dim(vw_pad, start, WINDOW + Q_BLK, axis=1)
        return nsa_attend(qi, gi, start + jnp.arange(Q_BLK), *ctx, kw, vw, start - WINDOW + band)

    out = lax.map(block, (qb, gb, jnp.arange(n_blk)))
    return out.swapaxes(0, 1).reshape(B, T, H * Dh)


def nsa_sample(q, gates, ctx, k_win, v_win, pos0, w_first):
    B, S, H, Dh = q.shape
    w_pos = w_first + jnp.arange(k_win.shape[1])
    q_pos = pos0 + jnp.arange(S)

    def step(args):
        qi, gi, t = args
        return nsa_attend(qi[:, None], gi[:, None], t[None], *ctx, k_win, v_win, w_pos)

    out = lax.map(step, (q.swapaxes(0, 1), gates.swapaxes(0, 1), q_pos))
    return out[:, :, 0].swapaxes(0, 1)


def kernel(x_prompt, x_sample, cache_kv, cache_win, cache_mem_kv, state_rg_h, state_rg_conv,
           page_table, mem_prompt, ln_g, ln_b, w_in_a, conv_w, conv_b, w_rg_a, b_rg_a,
           w_rg_i, b_rg_i, rg_lambda, w_out_a, w_kv_shared, cmp_pe, cmp_w1, cmp_b1, cmp_w2,
           cmp_b2, w_in_b, w_out_b, w_mem_kv, w_router, b_router, w_gate_up, b_gate_up,
           w_down, b_down):
    hq = N_HEADS * HEAD_DIM
    Bp, Tp, D = x_prompt.shape
    Bs, Ss, _ = x_sample.shape
    Np, Ns = Bp * Tp, Bs * Ss
    past_len = page_table.shape[1] * PAGE_SIZE
    w_buf = cache_win.shape[1]
    cmp_params = (cmp_pe, cmp_w1, cmp_b1, cmp_w2, cmp_b2)

    def split(t):
        return t[:Np].reshape(Bp, Tp, -1), t[Np:].reshape(Bs, Ss, -1)

    def join(tp, ts):
        return jnp.concatenate([tp.reshape(Np, -1), ts.reshape(Ns, -1)], axis=0)

    w_mem_all = w_mem_kv.transpose(1, 0, 2).reshape(D, DEPTH * 2 * MEM_W).astype(BF16)
    mem_all = dense(mem_prompt.reshape(Bp * N_MEM, D), w_mem_all, tm=Bp * N_MEM // 2)
    mem_kv_prompt = mem_all.reshape(Bp, N_MEM, DEPTH, 2, MEM_HEADS, MEM_HD).transpose(2, 0, 1, 3, 4, 5)

    x = join(x_prompt, x_sample)
    hs_p, hs_s, bufs_p, bufs_s = [], [], [], []
    for l in range(DEPTH):
        if l < N_A:
            p = dense(x, w_in_a[l].astype(BF16))
            pp, ps = split(p)
            mixed = []
            for grp, pg, mem_kv, conv_buf, h0 in (
                    (0, pp, mem_kv_prompt[l], jnp.zeros((Bp, CONV_W - 1, D_RNN), F32),
                     jnp.zeros((Bp, D_RNN), F32)),
                    (1, ps, cache_mem_kv[l], state_rg_conv[l], state_rg_h[l])):
                gate = jax.nn.gelu(pg[..., :D_RNN])
                u = pg[..., D_RNN:2 * D_RNN]
                q_mem = pg[..., 2 * D_RNN:]
                h, h_last, buf = rglru_branch(u, conv_buf, h0, conv_w[l], conv_b[l], w_rg_a[l],
                                              b_rg_a[l], w_rg_i[l], b_rg_i[l], rg_lambda[l])
                mixed.append(jnp.concatenate([h * gate, mem_attend(q_mem, mem_kv)], axis=-1))
                (hs_p, hs_s)[grp].append(h_last)
                (bufs_p, bufs_s)[grp].append(buf)
            x = dense_post_norm(join(*mixed), w_out_a[l].astype(BF16), x, ln_g[l, 0], ln_b[l, 0])
        else:
            jb = l - N_A
            if l == N_A:
                kv = dense(x, w_kv_shared.astype(BF16))
                kv_p, kv_s = split(kv)
                kv_p = kv_p.reshape(Bp, Tp, 6, N_KV, HEAD_DIM)
                kv_s = kv_s.reshape(Bs, Ss, 6, N_KV, HEAD_DIM)
                ctx_p = nsa_context(kv_p[:, :, :4], *cmp_params)
                past = cache_kv[page_table].reshape(Bs, past_len, 4, N_KV, HEAD_DIM)
                ctx_s = nsa_context(jnp.concatenate([past, kv_s[:, :, :4]], axis=1), *cmp_params)
                win_s = jnp.concatenate([cache_win, kv_s[:, :, 4:]], axis=1)
                ks, kw, kcv, covt = nsa_prompt_operands(kv_p, ctx_p[0], ctx_p[1])
            hpg = hq // N_KV
            wq = w_in_b[jb][:, :hq].reshape(D, N_KV, hpg)
            wg = w_in_b[jb][:, hq:hq + 3 * N_HEADS].reshape(D, N_KV, 3 * HEADS_PER_KV)
            slabs = jnp.concatenate(
                [wq, wg, jnp.zeros((D, N_KV, SLAB - hpg - 3 * HEADS_PER_KV), F32)], axis=-1)
            w_in = jnp.concatenate([slabs.reshape(D, N_KV * SLAB), w_in_b[jb][:, hq + 3 * N_HEADS:]],
                                   axis=1).astype(BF16)
            p = dense(x, w_in)
            o_nsa_p = nsa_prompt_pallas(p, ks, kw, kcv, covt, Bp, Tp, ctx_p[0].shape[1])
            ps = p[Np:, :N_KV * SLAB].reshape(Bs, Ss, N_KV, SLAB)
            q_s = ps[..., :hpg].reshape(Bs, Ss, N_HEADS, HEAD_DIM)
            gates_s = jax.nn.sigmoid(ps[..., hpg:hpg + 3 * HEADS_PER_KV]).reshape(Bs, Ss, N_HEADS, 3)
            o_s = nsa_sample(q_s, gates_s, ctx_s, win_s[:, :, 0], win_s[:, :, 1],
                             past_len, past_len - w_buf)
            o_nsa_s = jnp.pad(o_s.reshape(Ns, N_KV, hpg), ((0, 0), (0, 0), (0, SLAB - hpg)))
            o_nsa = jnp.concatenate([o_nsa_p, o_nsa_s.transpose(1, 0, 2)], axis=1)
            qm_p, qm_s = split(p[:, N_KV * SLAB:])
            o_mem = join(mem_attend(qm_p, mem_kv_prompt[l]), mem_attend(qm_s, cache_mem_kv[l]))
            x = dense_post_norm_b(o_nsa, o_mem, w_out_b[jb], x, ln_g[l, 0], ln_b[l, 0])
        x = moe_post_norm(x, w_router[l], b_router[l], w_gate_up[l], b_gate_up[l], w_down[l],
                          b_down[l], ln_g[l, 1], ln_b[l, 1])

    y_prompt, y_sample = split(x)
    kv_rows_prompt = kv_p[:, :, :4]
    kv_rows_sample = kv_s[:, :, :4]
    win_prompt = kv_p[:, Tp - min(WINDOW, Tp):, 4:]
    win_sample = win_s[:, win_s.shape[1] - min(WINDOW, past_len + Ss):]
    return (y_prompt, y_sample, kv_rows_prompt, kv_rows_sample, win_prompt, win_sample,
            mem_kv_prompt, jnp.stack(hs_p), jnp.stack(hs_s), jnp.stack(bufs_p), jnp.stack(bufs_s))
```

```python
import functools

import jax
import jax.numpy as jnp
from jax import lax
from jax.experimental import pallas as pl
from jax.experimental.pallas import tpu as pltpu

D_MODEL = 1024
DEPTH = 4
PAGE_SIZE = 128
N_A = DEPTH // 2
D_RNN = 768
N_RG_BLOCKS = 8
RG_BW = D_RNN // N_RG_BLOCKS
CONV_W = 4
RG_C = 8.0
N_HEADS = 12
N_KV = 4
HEAD_DIM = 64
CMP_LEN = 32
CMP_STRIDE = 16
CMP_HID = 128
SEL_BLK = 64
N_SEL = 16
WINDOW = 512
Q_BLK = 64
N_MEM = 256
MEM_HEADS = 4
MEM_HD = 64
MEM_W = MEM_HEADS * MEM_HD
N_EXPERTS = 32
TOP_K = 4
D_EXPERT = 1024
SWIGLU_LIMIT = 7.0
SWIGLU_ALPHA = 1.702
DN_ALPHA = (2 * DEPTH) ** 0.25
LN_EPS = 1e-5

F32 = jnp.float32
BF16 = jnp.bfloat16

ROW_TILE = 512
MOE_BLK = 256
LANES = 128
VMEM_LIMIT = 56 * 1024 * 1024


def _params(*sem):
    return pltpu.CompilerParams(dimension_semantics=sem, vmem_limit_bytes=VMEM_LIMIT)


def _dense_body(x_ref, w_ref, o_ref):
    o_ref[...] = jnp.dot(x_ref[...].astype(BF16), w_ref[...], preferred_element_type=F32)


def dense(x, w_bf16, tm=ROW_TILE):
    M, K = x.shape
    N = w_bf16.shape[1]
    return pl.pallas_call(
        _dense_body,
        grid=(M // tm,),
        in_specs=[pl.BlockSpec((tm, K), lambda i: (i, 0)),
                  pl.BlockSpec((K, N), lambda i: (0, 0))],
        out_specs=pl.BlockSpec((tm, N), lambda i: (i, 0)),
        out_shape=jax.ShapeDtypeStruct((M, N), F32),
        compiler_params=_params("parallel"),
        name="dense",
    )(x, w_bf16)


def _layer_norm_rows(z, g, b):
    mu = jnp.mean(z, axis=-1, keepdims=True)
    zc = z - mu
    var = jnp.mean(zc * zc, axis=-1, keepdims=True)
    return zc * lax.rsqrt(var + LN_EPS) * g + b


def _dense_ln_body(m_ref, w_ref, x_ref, g_ref, b_ref, o_ref):
    y = jnp.dot(m_ref[...].astype(BF16), w_ref[...], preferred_element_type=F32)
    o_ref[...] = _layer_norm_rows(DN_ALPHA * x_ref[...] + y, g_ref[...], b_ref[...])


def dense_post_norm(merged, w_bf16, x, g, b, tm=ROW_TILE):
    M, K = merged.shape
    D = w_bf16.shape[1]
    return pl.pallas_call(
        _dense_ln_body,
        grid=(M // tm,),
        in_specs=[pl.BlockSpec((tm, K), lambda i: (i, 0)),
                  pl.BlockSpec((K, D), lambda i: (0, 0)),
                  pl.BlockSpec((tm, D), lambda i: (i, 0)),
                  pl.BlockSpec((1, D), lambda i: (0, 0)),
                  pl.BlockSpec((1, D), lambda i: (0, 0))],
        out_specs=pl.BlockSpec((tm, D), lambda i: (i, 0)),
        out_shape=jax.ShapeDtypeStruct((M, D), F32),
        compiler_params=_params("parallel"),
        name="dense_post_norm",
    )(merged, w_bf16, x, g.reshape(1, D), b.reshape(1, D))


def _dense_ln_b_body(a_ref, m_ref, wa_ref, wm_ref, x_ref, g_ref, b_ref, o_ref):
    y = jnp.dot(m_ref[...].astype(BF16), wm_ref[...], preferred_element_type=F32)
    for grp in range(N_KV):
        y = y + jnp.dot(a_ref[grp].astype(BF16), wa_ref[grp], preferred_element_type=F32)
    o_ref[...] = _layer_norm_rows(DN_ALPHA * x_ref[...] + y, g_ref[...], b_ref[...])


def dense_post_norm_b(o_nsa, o_mem, w_out, x, g, b, tm=ROW_TILE):
    G, M, S = o_nsa.shape
    D = w_out.shape[1]
    hq = N_HEADS * HEAD_DIM
    wa = w_out[:hq].reshape(G, hq // G, D)
    wa = jnp.pad(wa, ((0, 0), (0, S - hq // G), (0, 0))).astype(BF16)
    wm = w_out[hq:].astype(BF16)
    return pl.pallas_call(
        _dense_ln_b_body,
        grid=(M // tm,),
        in_specs=[pl.BlockSpec((G, tm, S), lambda i: (0, i, 0)),
                  pl.BlockSpec((tm, MEM_W), lambda i: (i, 0)),
                  pl.BlockSpec((G, S, D), lambda i: (0, 0, 0)),
                  pl.BlockSpec((MEM_W, D), lambda i: (0, 0)),
                  pl.BlockSpec((tm, D), lambda i: (i, 0)),
                  pl.BlockSpec((1, D), lambda i: (0, 0)),
                  pl.BlockSpec((1, D), lambda i: (0, 0))],
        out_specs=pl.BlockSpec((tm, D), lambda i: (i, 0)),
        out_shape=jax.ShapeDtypeStruct((M, D), F32),
        compiler_params=_params("parallel"),
        name="dense_post_norm_b",
    )(o_nsa, o_mem, wa, wm, x, g.reshape(1, D), b.reshape(1, D))


def _router_body(x_ref, w_ref, b_ref, o_ref):
    o_ref[...] = jnp.dot(x_ref[...], w_ref[...], preferred_element_type=F32,
                         precision=lax.Precision.HIGHEST) + b_ref[...]


def router_logits(x, w_r, b_r, tm=ROW_TILE):
    M, K = x.shape
    w = jnp.pad(w_r, ((0, 0), (0, LANES - N_EXPERTS)))
    b = jnp.pad(b_r, (0, LANES - N_EXPERTS)).reshape(1, LANES)
    out = pl.pallas_call(
        _router_body,
        grid=(M // tm,),
        in_specs=[pl.BlockSpec((tm, K), lambda i: (i, 0)),
                  pl.BlockSpec((K, LANES), lambda i: (0, 0)),
                  pl.BlockSpec((1, LANES), lambda i: (0, 0))],
        out_specs=pl.BlockSpec((tm, LANES), lambda i: (i, 0)),
        out_shape=jax.ShapeDtypeStruct((M, LANES), F32),
        compiler_params=_params("parallel"),
        name="router",
    )(x, w, b)
    return out[:, :N_EXPERTS]


def _experts_body(tile_ref, e_ref, lo_ref, hi_ref, first_ref, x_ref, wr_ref, wgu_ref, bgu_ref,
                  wdn_ref, bdn_ref, o_ref, wgu_bf, wdn_bf):
    i = pl.program_id(0)
    e = e_ref[i]
    e_prev = e_ref[jnp.maximum(i - 1, 0)]
    lo, hi = lo_ref[i], hi_ref[i]

    @pl.when((i == 0) | (e != e_prev))
    def _():
        wgu_bf[...] = wgu_ref[0].astype(BF16)
        wdn_bf[...] = wdn_ref[0].astype(BF16)

    @pl.when(hi > lo)
    def _():
        h = jnp.dot(x_ref[...].astype(BF16), wgu_bf[...], preferred_element_type=F32) + bgu_ref[0]
        gate = jnp.minimum(h[:, :D_EXPERT], SWIGLU_LIMIT)
        up = jnp.clip(h[:, D_EXPERT:], -SWIGLU_LIMIT, SWIGLU_LIMIT)
        act = (up + 1.0) * gate * jax.nn.sigmoid(SWIGLU_ALPHA * gate)
        y = jnp.dot(act.astype(BF16), wdn_bf[...], preferred_element_type=F32) + bdn_ref[0]
        row = lax.broadcasted_iota(jnp.int32, (MOE_BLK, 1), 0)
        mine = (row >= lo) & (row < hi)
        y = y * wr_ref[...]

        @pl.when(first_ref[i] == 1)
        def _():
            o_ref[...] = jnp.where(mine, y, 0.0)

        @pl.when(first_ref[i] == 0)
        def _():
            o_ref[...] = jnp.where(mine, y, o_ref[...])


def experts(x_sorted, w_sorted, visits, w_gu, b_gu, w_dn, b_dn):
    R, D = x_sorted.shape
    n_vis = visits[0].shape[0]
    grid_spec = pltpu.PrefetchScalarGridSpec(
        num_scalar_prefetch=5,
        grid=(n_vis,),
        in_specs=[
            pl.BlockSpec((MOE_BLK, D), lambda i, t, e, lo, hi, f: (t[i], 0)),
            pl.BlockSpec((MOE_BLK, 1), lambda i, t, e, lo, hi, f: (t[i], 0)),
            pl.BlockSpec((1, D, 2 * D_EXPERT), lambda i, t, e, lo, hi, f: (e[i], 0, 0)),
            pl.BlockSpec((1, 1, 2 * D_EXPERT), lambda i, t, e, lo, hi, f: (e[i], 0, 0)),
            pl.BlockSpec((1, D_EXPERT, D), lambda i, t, e, lo, hi, f: (e[i], 0, 0)),
            pl.BlockSpec((1, 1, D), lambda i, t, e, lo, hi, f: (e[i], 0, 0)),
        ],
        out_specs=pl.BlockSpec((MOE_BLK, D), lambda i, t, e, lo, hi, f: (t[i], 0)),
        scratch_shapes=[pltpu.VMEM((D, 2 * D_EXPERT), BF16), pltpu.VMEM((D_EXPERT, D), BF16)],
    )
    return pl.pallas_call(
        _experts_body,
        grid_spec=grid_spec,
        out_shape=jax.ShapeDtypeStruct((R, D), F32),
        compiler_params=_params("arbitrary"),
        name="experts",
    )(*visits, x_sorted, w_sorted, w_gu, b_gu.reshape(N_EXPERTS, 1, -1), w_dn,
      b_dn.reshape(N_EXPERTS, 1, -1))


def _combine_ln_body(y_ref, x_ref, g_ref, b_ref, o_ref):
    D = x_ref.shape[1]
    y = y_ref[:, 0:D]
    for k in range(1, TOP_K):
        y = y + y_ref[:, k * D:(k + 1) * D]
    o_ref[...] = _layer_norm_rows(DN_ALPHA * x_ref[...] + y, g_ref[...], b_ref[...])


def combine_post_norm(y_assign, x, g, b, tm=ROW_TILE):
    M, D = x.shape
    return pl.pallas_call(
        _combine_ln_body,
        grid=(M // tm,),
        in_specs=[pl.BlockSpec((tm, TOP_K * D), lambda i: (i, 0)),
                  pl.BlockSpec((tm, D), lambda i: (i, 0)),
                  pl.BlockSpec((1, D), lambda i: (0, 0)),
                  pl.BlockSpec((1, D), lambda i: (0, 0))],
        out_specs=pl.BlockSpec((tm, D), lambda i: (i, 0)),
        out_shape=jax.ShapeDtypeStruct((M, D), F32),
        compiler_params=_params("parallel"),
        name="combine_post_norm",
    )(y_assign, x, g.reshape(1, D), b.reshape(1, D))


def expert_visits(starts, ends, n_tiles):
    n_vis = n_tiles + N_EXPERTS
    sizes = ends - starts
    first_tile = starts // MOE_BLK
    n_e = jnp.where(sizes > 0, (ends - 1) // MOE_BLK - first_tile + 1, 0)
    v_end = jnp.cumsum(n_e)
    v_start = v_end - n_e
    total = v_end[-1]
    v = jnp.arange(n_vis)
    real = v < total
    vc = jnp.minimum(v, total - 1)
    e_v = jnp.minimum(jnp.searchsorted(v_end, vc, side='right'), N_EXPERTS - 1)
    tile_v = first_tile[e_v] + vc - v_start[e_v]
    lo = jnp.clip(starts[e_v] - tile_v * MOE_BLK, 0, MOE_BLK)
    hi = jnp.where(real, jnp.clip(ends[e_v] - tile_v * MOE_BLK, 0, MOE_BLK), lo)
    first = real & ((v == 0) | (tile_v != jnp.roll(tile_v, 1)))
    return tuple(a.astype(jnp.int32) for a in (tile_v, e_v, lo, hi, first))


def moe_post_norm(x, w_r, b_r, w_gu, b_gu, w_dn, b_dn, g, b):
    N, D = x.shape
    logits = router_logits(x, w_r, b_r)
    top_v, top_e = lax.top_k(logits, TOP_K)
    wts = jax.nn.softmax(top_v, axis=-1)
    NK = N * TOP_K
    ids = jnp.arange(NK, dtype=jnp.int32)
    e_sorted, order, w_sorted = lax.sort(
        (top_e.reshape(-1).astype(jnp.int32), ids, wts.reshape(-1)), num_keys=1, is_stable=True)
    _, pos = lax.sort((order, ids), num_keys=1)
    ends = jnp.searchsorted(e_sorted, jnp.arange(N_EXPERTS, dtype=jnp.int32), side='right')
    starts = jnp.concatenate([jnp.zeros((1,), ends.dtype), ends[:-1]])
    visits = expert_visits(starts, ends, NK // MOE_BLK)
    y_sorted = experts(x[order // TOP_K], w_sorted.reshape(NK, 1), visits, w_gu, b_gu, w_dn, b_dn)
    return combine_post_norm(y_sorted[pos].reshape(N, TOP_K * D), x, g, b)


NSA_TQ = 128
NSA_TK = 512
NSA_TW = 128
HEADS_PER_KV = N_HEADS // N_KV
MASK_OFF = -(2.0 ** 30)
SLAB = 2 * LANES


def _flash_step(s, v, m_ref, l_ref, acc_ref):
    m_old = m_ref[...]
    m_new = jnp.maximum(m_old, jnp.max(s, axis=1, keepdims=True))
    alpha = jnp.exp(m_old - m_new)
    p = jnp.exp(s - m_new)
    l_ref[...] = alpha * l_ref[...] + jnp.sum(p, axis=1, keepdims=True)
    acc_ref[...] = alpha * acc_ref[...] + jnp.dot(p.astype(BF16), v, preferred_element_type=F32)
    m_ref[...] = m_new


def _nt_dot(a, b):
    return lax.dot_general(a, b, (((1,), (1,)), ((), ())), preferred_element_type=F32)


def _nsa_prompt_body(p_ref, ks_ref, kw_ref, kc_ref, covt_ref, o_ref,
                     qaug_ref, m_ref, l_ref, acc_ref, *, n_cmp, n_sel):
    TQ, R = NSA_TQ, HEADS_PER_KV
    i = pl.program_id(2)
    q0 = i * TQ
    slab = p_ref[...]
    a, bv = slab[:, :LANES], slab[:, LANES:]
    lane = lax.broadcasted_iota(jnp.int32, (TQ, LANES), 1)
    lo = lane < HEAD_DIM
    scale = HEAD_DIM ** -0.5
    q_heads = (jnp.where(lo, a, 0.0), jnp.where(lo, pltpu.roll(a, HEAD_DIM, 1), 0.0),
               jnp.where(lo, bv, 0.0))
    q3 = (jnp.concatenate(q_heads, axis=0) * scale).astype(BF16)
    row_q = q0 + lax.broadcasted_iota(jnp.int32, (TQ, 1), 0)
    row_q3 = jnp.concatenate([row_q] * R, axis=0)

    kc = kc_ref[0, 0]
    nc = kc.shape[0]
    s = _nt_dot(q3, kc)
    n_idx = lax.broadcasted_iota(jnp.int32, (R * TQ, nc), 1)
    vis = (n_idx * CMP_STRIDE + (CMP_LEN - 1) <= row_q3) & (n_idx < n_cmp)
    s = jnp.where(vis, s, MASK_OFF)
    m = jnp.max(s, axis=1, keepdims=True)
    p = jnp.where(vis, jnp.exp(s - m), 0.0)
    d = jnp.sum(p, axis=1, keepdims=True)
    p_c = p / jnp.where(d > 0, d, 1.0)
    o_c = jnp.dot(p_c.astype(BF16), kc, preferred_element_type=F32)

    p_sum = p_c[0:TQ]
    for r in range(1, R):
        p_sum = p_sum + p_c[r * TQ:(r + 1) * TQ]
    p_hi = p_sum.astype(BF16)
    p_lo = (p_sum - p_hi.astype(F32)).astype(BF16)
    covt = covt_ref[...]
    imp = _nt_dot(covt, p_hi) + _nt_dot(covt, p_lo)
    j = lax.broadcasted_iota(jnp.int32, (n_sel, TQ), 0)
    cur = jnp.right_shift(q0 + lax.broadcasted_iota(jnp.int32, (n_sel, TQ), 1),
                          SEL_BLK.bit_length() - 1)
    forced = (j == 0) | (j == cur) | (j == cur - 1)
    imp = jnp.where(forced, jnp.inf, jnp.where(j > cur, -jnp.inf, imp))
    rank = jnp.zeros((n_sel, TQ), F32)
    for b in range(n_sel):
        row = imp[b:b + 1, :]
        ahead = (row > imp) | ((row == imp) & (j > b))
        rank = rank + jnp.where(ahead, 1.0, 0.0)
    sel_off = jnp.where(rank < float(min(N_SEL, n_sel)), 0.0, MASK_OFF)
    sel_off = jnp.concatenate([sel_off, jnp.zeros((LANES - n_sel, TQ), F32)], axis=0)
    mask_q = sel_off.T.astype(BF16)
    qaug_ref[:, :LANES] = q3
    qaug_ref[:, LANES:] = jnp.concatenate([mask_q] * R, axis=0)

    def reset():
        m_ref[...] = jnp.full(m_ref.shape, MASK_OFF, F32)
        l_ref[...] = jnp.zeros(l_ref.shape, F32)
        acc_ref[...] = jnp.zeros(acc_ref.shape, F32)

    def finish():
        return acc_ref[...] / l_ref[...]

    reset()
    n_full = q0 // NSA_TK

    def sel_tile(t, masked):
        k0 = pl.multiple_of(t * NSA_TK, NSA_TK)
        kt = ks_ref[0, 0, pl.ds(k0, NSA_TK), :]
        s = _nt_dot(qaug_ref[...], kt)
        if masked:
            k_pos = k0 + lax.broadcasted_iota(jnp.int32, (R * TQ, NSA_TK), 1)
            s = jnp.where(k_pos <= row_q3, s, MASK_OFF)
        _flash_step(s, kt[:, :LANES], m_ref, l_ref, acc_ref)

    def sel_loop(t, c):
        sel_tile(t, False)
        return c

    lax.fori_loop(0, n_full, sel_loop, 0)
    sel_tile(n_full, True)
    o_s = finish()

    reset()
    w0 = jnp.maximum(q0 - WINDOW, 0)
    n_w = (q0 + TQ - w0) // NSA_TW

    def win_loop(u, c):
        k0 = pl.multiple_of(w0 + u * NSA_TW, NSA_TW)
        kt = kw_ref[0, 0, pl.ds(k0, NSA_TW), :]
        s = _nt_dot(q3, kt)
        dist = row_q3 - (k0 + lax.broadcasted_iota(jnp.int32, (R * TQ, NSA_TW), 1))
        s = jnp.where((dist >= 0) & (dist <= WINDOW), s, MASK_OFF)
        _flash_step(s, kt, m_ref, l_ref, acc_ref)
        return c

    lax.fori_loop(0, n_w, win_loop, 0)
    o_w = finish()

    gates = jax.nn.sigmoid(bv)
    outs = []
    for r in range(R):
        c0 = HEAD_DIM + 3 * r
        rows = slice(r * TQ, (r + 1) * TQ)
        outs.append(gates[:, c0:c0 + 1] * o_c[rows] + gates[:, c0 + 1:c0 + 2] * o_s[rows]
                    + gates[:, c0 + 2:c0 + 3] * o_w[rows])
    o_ref[0, :, :LANES] = jnp.where(lo, pltpu.roll(outs[0], HEAD_DIM, 1), outs[1])
    o_ref[0, :, LANES:] = jnp.where(lo, pltpu.roll(outs[2], HEAD_DIM, 1), 0.0)


def nsa_prompt_pallas(p, ks, kw, kc, covt, B, T, n_cmp):
    n_sel = covt.shape[0]
    nq = T // NSA_TQ
    rows = HEADS_PER_KV * NSA_TQ
    body = functools.partial(_nsa_prompt_body, n_cmp=n_cmp, n_sel=n_sel)
    return pl.pallas_call(
        body,
        grid=(B, N_KV, nq),
        in_specs=[pl.BlockSpec((NSA_TQ, SLAB), lambda b, g, i: (b * nq + i, g)),
                  pl.BlockSpec((1, 1, T, SLAB), lambda b, g, i: (b, g, 0, 0)),
                  pl.BlockSpec((1, 1, T, LANES), lambda b, g, i: (b, g, 0, 0)),
                  pl.BlockSpec((1, 1) + kc.shape[2:], lambda b, g, i: (b, g, 0, 0)),
                  pl.BlockSpec(covt.shape, lambda b, g, i: (0, 0))],
        out_specs=pl.BlockSpec((1, NSA_TQ, SLAB), lambda b, g, i: (g, b * nq + i, 0)),
        out_shape=jax.ShapeDtypeStruct((N_KV, B * T, SLAB), F32),
        scratch_shapes=[pltpu.VMEM((rows, SLAB), BF16), pltpu.VMEM((rows, 1), F32),
                        pltpu.VMEM((rows, 1), F32), pltpu.VMEM((rows, LANES), F32)],
        compiler_params=_params("parallel", "parallel", "arbitrary"),
        name="nsa_prompt",
    )(p, ks, kw, kc, covt)


def nsa_prompt_operands(kv_p, kc_all):
    B, T = kv_p.shape[:2]
    n_sel = T // SEL_BLK
    n_chunk = T // CMP_STRIDE
    n_cmp = n_chunk - CMP_R + 1
    kvt = kv_p.transpose(2, 0, 3, 1, 4).astype(BF16)
    e_t = (jnp.arange(T)[:, None] // SEL_BLK == jnp.arange(LANES)[None, :]).astype(BF16)
    e_t = jnp.broadcast_to(e_t, (B, N_KV, T, LANES))
    ks = jnp.concatenate([kvt[2], kvt[3], e_t], axis=-1)
    kw = jnp.concatenate([kvt[4], kvt[5]], axis=-1)
    kcv = kc_all.reshape(2, B, n_chunk, N_KV, HEAD_DIM).transpose(1, 3, 2, 0, 4)
    kcv = kcv.reshape(B, N_KV, n_chunk, 2 * HEAD_DIM).astype(BF16)
    c_start = jnp.arange(n_chunk) * CMP_STRIDE
    s_start = jnp.arange(n_sel) * SEL_BLK
    covt = ((c_start[None, :] < s_start[:, None] + SEL_BLK)
            & (c_start[None, :] + CMP_LEN - 1 >= s_start[:, None])
            & (jnp.arange(n_chunk)[None, :] < n_cmp)).astype(BF16)
    return ks, kw, kcv, covt


CMP_R = CMP_LEN // CMP_STRIDE
KV_W = N_KV * HEAD_DIM


def _compress_body(xa_ref, xb_ref, w1_ref, pe_ref, b1_ref, w2_ref, b2_ref, o_ref):
    m = o_ref.shape[1]
    acc = [None] * CMP_R
    for s in range(CMP_STRIDE):
        rows_s = pl.ds(s, m, stride=CMP_STRIDE)
        xs = jnp.concatenate([xa_ref[rows_s, :], xb_ref[rows_s, :]], axis=1)
        for r in range(CMP_R):
            t = jnp.dot((xs + pe_ref[0, r, s:s + 1, :]).astype(BF16), w1_ref[0, r, s],
                        preferred_element_type=F32)
            acc[r] = t if acc[r] is None else acc[r] + t
    hid = b1_ref[0] + acc[0]
    for r in range(1, CMP_R):
        hid = hid + pltpu.roll(acc[r], m - r, 0)
    o_ref[0] = jnp.dot(jax.nn.gelu(hid).astype(BF16), w2_ref[0],
                       preferred_element_type=F32) + b2_ref[0]


def compress_operands(cmp_pe, cmp_w1, cmp_b1, cmp_w2, cmp_b2):
    eye = jnp.eye(N_KV, dtype=F32)
    w1 = cmp_w1.reshape(2, CMP_R, CMP_STRIDE, HEAD_DIM, CMP_HID)
    w1 = jnp.einsum('krsdh,gG->krsgdGh', w1, eye).reshape(
        2, CMP_R, CMP_STRIDE, KV_W, N_KV * CMP_HID).astype(BF16)
    pe = jnp.tile(cmp_pe.reshape(2, CMP_R, CMP_STRIDE, HEAD_DIM), (1, 1, 1, N_KV))
    b1 = jnp.tile(cmp_b1, (1, N_KV)).reshape(2, 1, N_KV * CMP_HID)
    w2 = jnp.einsum('khd,gG->kghGd', cmp_w2, eye).reshape(2, N_KV * CMP_HID, KV_W).astype(BF16)
    b2 = jnp.tile(cmp_b2, (1, N_KV)).reshape(2, 1, KV_W)
    return w1, pe, b1, w2, b2


def compress(x2d, n_seq, seq_rows, seqs_per_step, ops):
    w1, pe, b1, w2, b2 = ops
    rows = seqs_per_step * seq_rows
    m = rows // CMP_STRIDE
    return pl.pallas_call(
        _compress_body,
        grid=(2, n_seq // seqs_per_step),
        in_specs=[pl.BlockSpec((rows, LANES), lambda k, i: (i, 2 * k)),
                  pl.BlockSpec((rows, LANES), lambda k, i: (i, 2 * k + 1)),
                  pl.BlockSpec((1,) + w1.shape[1:], lambda k, i: (k, 0, 0, 0, 0)),
                  pl.BlockSpec((1,) + pe.shape[1:], lambda k, i: (k, 0, 0, 0)),
                  pl.BlockSpec((1,) + b1.shape[1:], lambda k, i: (k, 0, 0)),
                  pl.BlockSpec((1,) + w2.shape[1:], lambda k, i: (k, 0, 0)),
                  pl.BlockSpec((1,) + b2.shape[1:], lambda k, i: (k, 0, 0))],
        out_specs=pl.BlockSpec((1, m, KV_W), lambda k, i: (k, i, 0)),
        out_shape=jax.ShapeDtypeStruct((2, n_seq * seq_rows // CMP_STRIDE, KV_W), F32),
        compiler_params=_params("arbitrary", "arbitrary"),
        name="compress",
    )(x2d, x2d, w1, pe, b1, w2, b2)


def _softmax_two(s_a, s_b):
    m = jnp.maximum(jnp.max(s_a, axis=1, keepdims=True), jnp.max(s_b, axis=1, keepdims=True))
    p_a, p_b = jnp.exp(s_a - m), jnp.exp(s_b - m)
    d = jnp.sum(p_a, axis=1, keepdims=True) + jnp.sum(p_b, axis=1, keepdims=True)
    return p_a, p_b, d


def _nsa_sample_body(p_ref, past_ref, new_ref, win_ref, kc_ref, cov_ref, esel_ref, o_ref,
                     *, past_len, n_cmp, n_sel):
    S = p_ref.shape[0]
    G, R = N_KV, HEADS_PER_KV
    w_buf = win_ref.shape[0]
    slab = p_ref[...]
    lane = lax.broadcasted_iota(jnp.int32, (S, LANES), 1)
    lo = lane < HEAD_DIM
    zero = jnp.zeros((S, LANES), F32)

    def half(x, src_hi, dst_hi):
        x = x if src_hi == dst_hi else pltpu.roll(x, HEAD_DIM, 1)
        return jnp.where(lo != dst_hi, x, 0.0)

    q_rows, gate_cols = [], ([], [], [])
    for g in range(G):
        a = slab[:, g * SLAB:g * SLAB + LANES]
        bv = slab[:, g * SLAB + LANES:(g + 1) * SLAB]
        sig = jax.nn.sigmoid(bv)
        for r in range(R):
            piece = half(a if r < 2 else bv, r == 1, g % 2 == 1)
            tiles = [zero, zero]
            tiles[g // 2] = piece
            q_rows.append(jnp.concatenate(tiles, axis=1))
            for br in range(3):
                c = HEAD_DIM + 3 * r + br
                gate_cols[br].append(sig[:, c:c + 1])
    q_all = (jnp.concatenate(q_rows, axis=0) * HEAD_DIM ** -0.5).astype(BF16)
    gate = [jnp.concatenate(cols, axis=0) for cols in gate_cols]
    n_rows = G * R * S
    t_col = lax.broadcasted_iota(jnp.int32, (S, 1), 0)
    t_row = jnp.concatenate([t_col] * (G * R), axis=0)
    q_pos = past_len + t_row

    kck = kc_ref[0].astype(BF16)
    kcv = kc_ref[1].astype(BF16)
    nc = kck.shape[0]
    s = _nt_dot(q_all, kck)
    n_idx = lax.broadcasted_iota(jnp.int32, (n_rows, nc), 1)
    vis = (n_idx * CMP_STRIDE + (CMP_LEN - 1) <= q_pos) & (n_idx < n_cmp)
    s = jnp.where(vis, s, MASK_OFF)
    m = jnp.max(s, axis=1, keepdims=True)
    p = jnp.where(vis, jnp.exp(s - m), 0.0)
    d = jnp.sum(p, axis=1, keepdims=True)
    p_c = p / jnp.where(d > 0, d, 1.0)
    o_c = jnp.dot(p_c.astype(BF16), kcv, preferred_element_type=F32)

    p_sum = []
    for g in range(G):
        acc = p_c[g * R * S:(g * R + 1) * S]
        for r in range(1, R):
            acc = acc + p_c[(g * R + r) * S:(g * R + r + 1) * S]
        p_sum.append(acc)
    p_sum = jnp.concatenate(p_sum, axis=0)
    p_hi = p_sum.astype(BF16)
    p_lo = (p_sum - p_hi.astype(F32)).astype(BF16)
    cov = cov_ref[...]
    imp = (jnp.dot(p_hi, cov, preferred_element_type=F32)
           + jnp.dot(p_lo, cov, preferred_element_type=F32))
    j = lax.broadcasted_iota(jnp.int32, (G * S, LANES), 1)
    cur = jnp.right_shift(past_len + jnp.concatenate([t_col] * G, axis=0), SEL_BLK.bit_length() - 1)
    forced = (j == 0) | (j == cur) | (j == cur - 1)
    imp = jnp.where(forced, jnp.inf, jnp.where((j > cur) | (j >= n_sel), -jnp.inf, imp))
    rank = jnp.zeros((G * S, LANES), F32)
    for b in range(n_sel):
        col = imp[:, b:b + 1]
        rank = rank + jnp.where((col > imp) | ((col == imp) & (j > b)), 1.0, 0.0)
    sel_off = jnp.where(rank < float(min(N_SEL, n_sel)), 0.0, MASK_OFF)
    mask_rows = jnp.concatenate(
        [sel_off[g * S:(g + 1) * S] for g in range(G) for _ in range(R)], axis=0).astype(BF16)

    new = new_ref[...]
    pad = jnp.zeros((LANES - S, KV_W), F32)
    u = lax.broadcasted_iota(jnp.int32, (n_rows, LANES), 1)
    new_vis = u <= t_row

    def new_keys(slot):
        return jnp.concatenate([new[:, slot * KV_W:(slot + 1) * KV_W], pad], axis=0).astype(BF16)

    esel = esel_ref[...]
    s_p = (_nt_dot(q_all, past_ref[:, :KV_W].astype(BF16))
           + jnp.dot(mask_rows, esel[:, :past_len], preferred_element_type=F32))
    s_n = (_nt_dot(q_all, new_keys(2))
           + jnp.dot(mask_rows, esel[:, past_len:], preferred_element_type=F32))
    s_n = jnp.where(new_vis, s_n, MASK_OFF)
    p_p, p_n, d = _softmax_two(s_p, s_n)
    o_s = (jnp.dot(p_p.astype(BF16), past_ref[:, KV_W:].astype(BF16), preferred_element_type=F32)
           + jnp.dot(p_n.astype(BF16), new_keys(3), preferred_element_type=F32)) / d

    s_w = _nt_dot(q_all, win_ref[:, :KV_W].astype(BF16))
    w_pos = (past_len - w_buf) + lax.broadcasted_iota(jnp.int32, (n_rows, w_buf), 1)
    dist = q_pos - w_pos
    s_w = jnp.where((w_pos >= 0) & (dist >= 0) & (dist <= WINDOW), s_w, MASK_OFF)
    s_wn = jnp.where(new_vis, _nt_dot(q_all, new_keys(4)), MASK_OFF)
    p_w, p_wn, d = _softmax_two(s_w, s_wn)
    o_w = (jnp.dot(p_w.astype(BF16), win_ref[:, KV_W:].astype(BF16), preferred_element_type=F32)
           + jnp.dot(p_wn.astype(BF16), new_keys(5), preferred_element_type=F32)) / d

    out = gate[0] * o_c + gate[1] * o_s + gate[2] * o_w

    for g in range(G):
        def head(r, dst_hi):
            rows = out[(g * R + r) * S:(g * R + r + 1) * S]
            return half(rows[:, (g // 2) * LANES:(g // 2 + 1) * LANES], g % 2 == 1, dst_hi)
        o_ref[:, g * SLAB:g * SLAB + LANES] = head(0, False) + head(1, True)
        o_ref[:, g * SLAB + LANES:(g + 1) * SLAB] = head(2, False)


def nsa_sample_pallas(p, row0, past2d, kv, cache_win2d, kc_all, Bs, S, past_len):
    w_buf = cache_win2d.shape[0] // Bs
    T = past_len + S
    n_sel = -(-T // SEL_BLK)
    nc = kc_all.shape[1] // Bs
    n_cmp = T // CMP_STRIDE - CMP_R + 1
    c_start = jnp.arange(nc) * CMP_STRIDE
    s_start = jnp.arange(LANES) * SEL_BLK
    cov = ((c_start[:, None] < s_start[None, :] + SEL_BLK)
           & (c_start[:, None] + CMP_LEN - 1 >= s_start[None, :])
           & (jnp.arange(nc)[:, None] < n_cmp) & (jnp.arange(LANES)[None, :] < n_sel)).astype(BF16)
    k_pos = jnp.arange(past_len + LANES)
    esel = ((k_pos[None, :] // SEL_BLK == jnp.arange(LANES)[:, None])
            & (k_pos[None, :] < T)).astype(BF16)
    body = functools.partial(_nsa_sample_body, past_len=past_len, n_cmp=n_cmp, n_sel=n_sel)
    blk0 = row0 // S
    return pl.pallas_call(
        body,
        grid=(Bs,),
        in_specs=[pl.BlockSpec((S, N_KV * SLAB), lambda b: (blk0 + b, 0)),
                  pl.BlockSpec((past_len, 2 * KV_W), lambda b: (b, 1)),
                  pl.BlockSpec((S, 6 * KV_W), lambda b: (blk0 + b, 0)),
                  pl.BlockSpec((w_buf, 2 * KV_W), lambda b: (b, 0)),
                  pl.BlockSpec((2, nc, KV_W), lambda b: (0, b, 0)),
                  pl.BlockSpec(cov.shape, lambda b: (0, 0)),
                  pl.BlockSpec(esel.shape, lambda b: (0, 0))],
        out_specs=pl.BlockSpec((S, N_KV * SLAB), lambda b: (b, 0)),
        out_shape=jax.ShapeDtypeStruct((Bs * S, N_KV * SLAB), F32),
        compiler_params=_params("parallel"),
        name="nsa_sample",
    )(p, past2d, kv, cache_win2d, kc_all, cov, esel)


def masked_softmax(s, mask):
    s = jnp.where(mask, s, -jnp.inf)
    m = jnp.max(s, axis=-1, keepdims=True)
    m = jnp.where(jnp.isfinite(m), m, 0.0)
    p = jnp.exp(s - m)
    d = jnp.sum(p, axis=-1, keepdims=True)
    return p / jnp.where(d > 0, d, 1.0)


def mem_attend(q, mem_kv):
    B, T, _ = q.shape
    qh = q.reshape(B, T, MEM_HEADS, MEM_HD)
    k, v = mem_kv[:, :, 0], mem_kv[:, :, 1]
    s = jnp.einsum('bthd,bmhd->bhtm', qh, k).astype(F32) * (MEM_HD ** -0.5)
    p = jax.nn.softmax(s, axis=-1)
    o = jnp.einsum('bhtm,bmhd->bthd', p.astype(v.dtype), v)
    return o.reshape(B, T, MEM_W)


def _lin_combine(c1, c2):
    a1, b1 = c1
    a2, b2 = c2
    return a1 * a2, a2 * b1 + b2


def rglru_branch(u, conv_buf, h0, conv_w, conv_b, w_a, b_a, w_i, b_i, lam):
    B, T, C = u.shape
    full = jnp.concatenate([conv_buf.astype(u.dtype), u], axis=1)
    xc = lax.conv_general_dilated(full, conv_w.astype(u.dtype)[:, None, :], (1,), 'VALID',
                                  dimension_numbers=('NWC', 'WIO', 'NWC'),
                                  feature_group_count=C) + conv_b
    xb = xc.reshape(B, T, N_RG_BLOCKS, RG_BW)
    r = jax.nn.sigmoid(jnp.einsum('btnc,ncd->btnd', xb, w_a).reshape(B, T, C) + b_a)
    i = jax.nn.sigmoid(jnp.einsum('btnc,ncd->btnd', xb, w_i).reshape(B, T, C) + b_i)
    log_a = -RG_C * r.astype(F32) * jax.nn.softplus(-lam.astype(F32))
    a = jnp.exp(log_a)
    b = jnp.sqrt(-jnp.expm1(2.0 * log_a)) * (i * xc).astype(F32)
    b = b.at[:, 0].add(a[:, 0] * h0.astype(F32))
    _, h = lax.associative_scan(_lin_combine, (a, b), axis=1)
    return h.astype(u.dtype), h[:, -1].astype(u.dtype), full[:, T:]


def compress_tokens(kseq, pe, w1, b1, w2, b2):
    B, T, G, Dh = kseq.shape
    R = CMP_LEN // CMP_STRIDE
    n_chunk = T // CMP_STRIDE
    n_cmp = n_chunk - R + 1
    ch = kseq[:, :n_chunk * CMP_STRIDE].reshape(B, n_chunk, CMP_STRIDE, G, Dh)
    pe_r = pe.reshape(R, CMP_STRIDE, Dh)
    w1_r = w1.reshape(R, CMP_STRIDE, Dh, CMP_HID)
    hid = b1
    for r in range(R):
        hid = hid + jnp.einsum('bnsgd,sdh->bngh', ch[:, r:r + n_cmp] + pe_r[r][:, None, :], w1_r[r])
    out = jnp.einsum('bngh,hd->bngd', jax.nn.gelu(hid), w2) + b2
    ends = jnp.arange(n_cmp) * CMP_STRIDE + (CMP_LEN - 1)
    return out, ends


def nsa_context(kv4, cmp_pe, cmp_w1, cmp_b1, cmp_w2, cmp_b2):
    kc, ends = compress_tokens(kv4[:, :, 0], cmp_pe[0], cmp_w1[0], cmp_b1[0], cmp_w2[0], cmp_b2[0])
    vc, _ = compress_tokens(kv4[:, :, 1], cmp_pe[1], cmp_w1[1], cmp_b1[1], cmp_w2[1], cmp_b2[1])
    B, T = kv4.shape[:2]
    n_sel = -(-T // SEL_BLK)
    sel = jnp.pad(kv4[:, :, 2:4], ((0, 0), (0, n_sel * SEL_BLK - T), (0, 0), (0, 0), (0, 0)))
    sel = sel.reshape(B, n_sel, SEL_BLK, 2, N_KV, HEAD_DIM).transpose(3, 0, 4, 1, 2, 5)
    return kc, vc, ends, sel[0], sel[1]


def nsa_attend(q, gates, q_pos, kc, vc, c_end, ksb, vsb, kw, vw, w_pos):
    B, Qb, H, Dh = q.shape
    G = kc.shape[2]
    R = H // G
    n_sel = ksb.shape[2]
    scale = Dh ** -0.5
    qg = q.reshape(B, Qb, G, R, Dh)
    s_c = jnp.einsum('bqgrd,bngd->bgrqn', qg, kc).astype(F32) * scale
    p_c = masked_softmax(s_c, c_end[None, :] <= q_pos[:, None])
    o_c = jnp.einsum('bgrqn,bngd->bqgrd', p_c.astype(vc.dtype), vc)
    c_start = c_end - (CMP_LEN - 1)
    s_start = jnp.arange(n_sel) * SEL_BLK
    cover = ((c_start[:, None] < s_start[None, :] + SEL_BLK) & (c_end[:, None] >= s_start[None, :])).astype(F32)
    imp = jnp.einsum('bgrqn,nj->bgqj', p_c, cover)
    cur = q_pos // SEL_BLK
    j = jnp.arange(n_sel)
    forced = (j[None, :] == 0) | (j[None, :] == cur[:, None]) | (j[None, :] == cur[:, None] - 1)
    future = j[None, :] > cur[:, None]
    imp = jnp.where(forced, jnp.inf, jnp.where(future, -jnp.inf, imp))
    _, idx = lax.top_k(imp, min(N_SEL, n_sel))
    gather = jax.vmap(jax.vmap(lambda blocks, ids: blocks[ids]))
    ks = gather(ksb, idx)
    vs = gather(vsb, idx)
    n_top = idx.shape[-1]
    k_pos = idx[..., None] * SEL_BLK + jnp.arange(SEL_BLK)
    m_s = (k_pos <= q_pos[:, None, None]).reshape(B, G, 1, Qb, n_top * SEL_BLK)
    s_s = jnp.einsum('bqgrd,bgqksd->bgrqks', qg, ks).astype(F32) * scale
    p_s = masked_softmax(s_s.reshape(B, G, R, Qb, n_top * SEL_BLK), m_s)
    o_s = jnp.einsum('bgrqm,bgqmd->bqgrd', p_s.astype(vs.dtype), vs.reshape(B, G, Qb, n_top * SEL_BLK, Dh))
    dist = q_pos[:, None] - w_pos[None, :]
    m_w = (w_pos[None, :] >= 0) & (dist >= 0) & (dist <= WINDOW)
    s_w = jnp.einsum('bqgrd,bngd->bgrqn', qg, kw).astype(F32) * scale
    p_w = masked_softmax(s_w, m_w)
    o_w = jnp.einsum('bgrqn,bngd->bqgrd', p_w.astype(vw.dtype), vw)
    g = gates.reshape(B, Qb, G, R, 3)
    out = g[..., 0:1] * o_c + g[..., 1:2] * o_s + g[..., 2:3] * o_w
    return out.reshape(B, Qb, H * Dh)


def nsa_prompt(q, gates, ctx, k_win, v_win):
    B, T, H, Dh = q.shape
    n_blk = T // Q_BLK
    zpad = jnp.zeros((B, WINDOW) + k_win.shape[2:], k_win.dtype)
    kw_pad = jnp.concatenate([zpad, k_win], axis=1)
    vw_pad = jnp.concatenate([zpad.astype(v_win.dtype), v_win], axis=1)
    qb = q.reshape(B, n_blk, Q_BLK, H, Dh).swapaxes(0, 1)
    gb = gates.reshape(B, n_blk, Q_BLK, H, 3).swapaxes(0, 1)
    band = jnp.arange(WINDOW + Q_BLK)

    def block(args):
        qi, gi, i = args
        start = i * Q_BLK
        kw = lax.dynamic_slice_in_dim(kw_pad, start, WINDOW + Q_BLK, axis=1)
        vw = lax.dynamic_slice_in_dim(vw_pad, start, WINDOW + Q_BLK, axis=1)
        return nsa_attend(qi, gi, start + jnp.arange(Q_BLK), *ctx, kw, vw, start - WINDOW + band)

    out = lax.map(block, (qb, gb, jnp.arange(n_blk)))
    return out.swapaxes(0, 1).reshape(B, T, H * Dh)


def nsa_sample(q, gates, ctx, k_win, v_win, pos0, w_first):
    B, S, H, Dh = q.shape
    w_pos = w_first + jnp.arange(k_win.shape[1])
    q_pos = pos0 + jnp.arange(S)

    def step(args):
        qi, gi, t = args
        return nsa_attend(qi[:, None], gi[:, None], t[None], *ctx, k_win, v_win, w_pos)

    out = lax.map(step, (q.swapaxes(0, 1), gates.swapaxes(0, 1), q_pos))
    return out[:, :, 0].swapaxes(0, 1)


def kernel(x_prompt, x_sample, cache_kv, cache_win, cache_mem_kv, state_rg_h, state_rg_conv,
           page_table, mem_prompt, ln_g, ln_b, w_in_a, conv_w, conv_b, w_rg_a, b_rg_a,
           w_rg_i, b_rg_i, rg_lambda, w_out_a, w_kv_shared, cmp_pe, cmp_w1, cmp_b1, cmp_w2,
           cmp_b2, w_in_b, w_out_b, w_mem_kv, w_router, b_router, w_gate_up, b_gate_up,
           w_down, b_down):
    hq = N_HEADS * HEAD_DIM
    Bp, Tp, D = x_prompt.shape
    Bs, Ss, _ = x_sample.shape
    Np, Ns = Bp * Tp, Bs * Ss
    past_len = page_table.shape[1] * PAGE_SIZE
    w_buf = cache_win.shape[1]
    cmp_params = (cmp_pe, cmp_w1, cmp_b1, cmp_w2, cmp_b2)

    def split(t):
        return t[:Np].reshape(Bp, Tp, -1), t[Np:].reshape(Bs, Ss, -1)

    def join(tp, ts):
        return jnp.concatenate([tp.reshape(Np, -1), ts.reshape(Ns, -1)], axis=0)

    w_mem_all = w_mem_kv.transpose(1, 0, 2).reshape(D, DEPTH * 2 * MEM_W).astype(BF16)
    mem_all = dense(mem_prompt.reshape(Bp * N_MEM, D), w_mem_all, tm=Bp * N_MEM // 2)
    mem_kv_prompt = mem_all.reshape(Bp, N_MEM, DEPTH, 2, MEM_HEADS, MEM_HD).transpose(2, 0, 1, 3, 4, 5)

    x = join(x_prompt, x_sample)
    hs_p, hs_s, bufs_p, bufs_s = [], [], [], []
    for l in range(DEPTH):
        if l < N_A:
            p = dense(x, w_in_a[l].astype(BF16))
            pp, ps = split(p)
            mixed = []
            for grp, pg, mem_kv, conv_buf, h0 in (
                    (0, pp, mem_kv_prompt[l], jnp.zeros((Bp, CONV_W - 1, D_RNN), F32),
                     jnp.zeros((Bp, D_RNN), F32)),
                    (1, ps, cache_mem_kv[l], state_rg_conv[l], state_rg_h[l])):
                gate = jax.nn.gelu(pg[..., :D_RNN])
                u = pg[..., D_RNN:2 * D_RNN]
                q_mem = pg[..., 2 * D_RNN:]
                h, h_last, buf = rglru_branch(u, conv_buf, h0, conv_w[l], conv_b[l], w_rg_a[l],
                                              b_rg_a[l], w_rg_i[l], b_rg_i[l], rg_lambda[l])
                mixed.append(jnp.concatenate([h * gate, mem_attend(q_mem, mem_kv)], axis=-1))
                (hs_p, hs_s)[grp].append(h_last)
                (bufs_p, bufs_s)[grp].append(buf)
            x = dense_post_norm(join(*mixed), w_out_a[l].astype(BF16), x, ln_g[l, 0], ln_b[l, 0])
        else:
            jb = l - N_A
            if l == N_A:
                kv = dense(x, w_kv_shared.astype(BF16))
                kv_p, kv_s = split(kv)
                kv_p = kv_p.reshape(Bp, Tp, 6, N_KV, HEAD_DIM)
                kv_s = kv_s.reshape(Bs, Ss, 6, N_KV, HEAD_DIM)
                cmp_ops = compress_operands(*cmp_params)
                kc_p = compress(kv, Bp, Tp, 1, cmp_ops)
                assert (past_len + Ss) // CMP_STRIDE * CMP_STRIDE == past_len
                past2d = cache_kv.reshape(-1, PAGE_SIZE, 4 * KV_W)[page_table].reshape(
                    Bs * past_len, 4 * KV_W)
                kc_s = compress(past2d, Bs, past_len, 2, cmp_ops)
                win_s = jnp.concatenate([cache_win, kv_s[:, :, 4:]], axis=1)
                ks, kw, kcv, covt = nsa_prompt_operands(kv_p, kc_p)
            hpg = hq // N_KV
            wq = w_in_b[jb][:, :hq].reshape(D, N_KV, hpg)
            wg = w_in_b[jb][:, hq:hq + 3 * N_HEADS].reshape(D, N_KV, 3 * HEADS_PER_KV)
            slabs = jnp.concatenate(
                [wq, wg, jnp.zeros((D, N_KV, SLAB - hpg - 3 * HEADS_PER_KV), F32)], axis=-1)
            w_in = jnp.concatenate([slabs.reshape(D, N_KV * SLAB), w_in_b[jb][:, hq + 3 * N_HEADS:]],
                                   axis=1).astype(BF16)
            p = dense(x, w_in)
            o_nsa_p = nsa_prompt_pallas(p, ks, kw, kcv, covt, Bp, Tp,
                                        Tp // CMP_STRIDE - CMP_R + 1)
            o_nsa_s = nsa_sample_pallas(p, Np, past2d, kv, cache_win.reshape(Bs * w_buf, 2 * KV_W),
                                        kc_s, Bs, Ss, past_len)
            o_nsa = jnp.concatenate(
                [o_nsa_p, o_nsa_s.reshape(Ns, N_KV, SLAB).transpose(1, 0, 2)], axis=1)
            qm_p, qm_s = split(p[:, N_KV * SLAB:])
            o_mem = join(mem_attend(qm_p, mem_kv_prompt[l]), mem_attend(qm_s, cache_mem_kv[l]))
            x = dense_post_norm_b(o_nsa, o_mem, w_out_b[jb], x, ln_g[l, 0], ln_b[l, 0])
        x = moe_post_norm(x, w_router[l], b_router[l], w_gate_up[l], b_gate_up[l], w_down[l],
                          b_down[l], ln_g[l, 1], ln_b[l, 1])

    y_prompt, y_sample = split(x)
    kv_rows_prompt = kv_p[:, :, :4]
    kv_rows_sample = kv_s[:, :, :4]
    win_prompt = kv_p[:, Tp - min(WINDOW, Tp):, 4:]
    win_sample = win_s[:, win_s.shape[1] - min(WINDOW, past_len + Ss):]
    return (y_prompt, y_sample, kv_rows_prompt, kv_rows_sample, win_prompt, win_sample,
            mem_kv_prompt, jnp.stack(hs_p), jnp.stack(hs_s), jnp.stack(bufs_p), jnp.stack(bufs_s))
```

```python
import functools

import jax
import jax.numpy as jnp
from jax import lax
from jax.experimental import pallas as pl
from jax.experimental.pallas import tpu as pltpu

D_MODEL = 1024
DEPTH = 4
PAGE_SIZE = 128
N_A = DEPTH // 2
D_RNN = 768
N_RG_BLOCKS = 8
RG_BW = D_RNN // N_RG_BLOCKS
CONV_W = 4
RG_C = 8.0
N_HEADS = 12
N_KV = 4
HEAD_DIM = 64
CMP_LEN = 32
CMP_STRIDE = 16
CMP_HID = 128
SEL_BLK = 64
N_SEL = 16
WINDOW = 512
Q_BLK = 64
N_MEM = 256
MEM_HEADS = 4
MEM_HD = 64
MEM_W = MEM_HEADS * MEM_HD
N_EXPERTS = 32
TOP_K = 4
D_EXPERT = 1024
SWIGLU_LIMIT = 7.0
SWIGLU_ALPHA = 1.702
DN_ALPHA = (2 * DEPTH) ** 0.25
LN_EPS = 1e-5

F32 = jnp.float32
BF16 = jnp.bfloat16

ROW_TILE = 512
MOE_BLK = 256
LANES = 128
VMEM_LIMIT = 56 * 1024 * 1024


def _params(*sem):
    return pltpu.CompilerParams(dimension_semantics=sem, vmem_limit_bytes=VMEM_LIMIT)


def _dense_body(x_ref, w_ref, o_ref):
    o_ref[...] = jnp.dot(x_ref[...].astype(BF16), w_ref[...], preferred_element_type=F32)


def dense(x, w_bf16, tm=ROW_TILE):
    M, K = x.shape
    N = w_bf16.shape[1]
    return pl.pallas_call(
        _dense_body,
        grid=(M // tm,),
        in_specs=[pl.BlockSpec((tm, K), lambda i: (i, 0)),
                  pl.BlockSpec((K, N), lambda i: (0, 0))],
        out_specs=pl.BlockSpec((tm, N), lambda i: (i, 0)),
        out_shape=jax.ShapeDtypeStruct((M, N), F32),
        compiler_params=_params("parallel"),
        name="dense",
    )(x, w_bf16)


def _layer_norm_rows(z, g, b):
    mu = jnp.mean(z, axis=-1, keepdims=True)
    zc = z - mu
    var = jnp.mean(zc * zc, axis=-1, keepdims=True)
    return zc * lax.rsqrt(var + LN_EPS) * g + b


def _dense_ln_body(m1_ref, m2_ref, w1_ref, w2_ref, x_ref, g_ref, b_ref, o_ref):
    y = (jnp.dot(m1_ref[...].astype(BF16), w1_ref[...], preferred_element_type=F32)
         + jnp.dot(m2_ref[...].astype(BF16), w2_ref[...], preferred_element_type=F32))
    o_ref[...] = _layer_norm_rows(DN_ALPHA * x_ref[...] + y, g_ref[...], b_ref[...])


def dense_post_norm(m1, m2, w, x, g, b, tm=ROW_TILE):
    M, K1 = m1.shape
    K2 = m2.shape[1]
    D = w.shape[1]
    row = lambda i: (i, 0)
    fixed = lambda i: (0, 0)
    return pl.pallas_call(
        _dense_ln_body,
        grid=(M // tm,),
        in_specs=[pl.BlockSpec((tm, K1), row), pl.BlockSpec((tm, K2), row),
                  pl.BlockSpec((K1, D), fixed), pl.BlockSpec((K2, D), fixed),
                  pl.BlockSpec((tm, D), row), pl.BlockSpec((1, D), fixed),
                  pl.BlockSpec((1, D), fixed)],
        out_specs=pl.BlockSpec((tm, D), row),
        out_shape=jax.ShapeDtypeStruct((M, D), F32),
        compiler_params=_params("parallel"),
        name="dense_post_norm",
    )(m1, m2, w[:K1].astype(BF16), w[K1:].astype(BF16), x, g.reshape(1, D), b.reshape(1, D))


def _dense_ln_b_body(a_ref, m_ref, wa_ref, wm_ref, x_ref, g_ref, b_ref, o_ref):
    y = jnp.dot(m_ref[...].astype(BF16), wm_ref[...], preferred_element_type=F32)
    for grp in range(N_KV):
        y = y + jnp.dot(a_ref[grp].astype(BF16), wa_ref[grp], preferred_element_type=F32)
    o_ref[...] = _layer_norm_rows(DN_ALPHA * x_ref[...] + y, g_ref[...], b_ref[...])


def dense_post_norm_b(o_nsa, o_mem, w_out, x, g, b, tm=ROW_TILE):
    G, M, S = o_nsa.shape
    D = w_out.shape[1]
    hq = N_HEADS * HEAD_DIM
    wa = w_out[:hq].reshape(G, hq // G, D)
    wa = jnp.pad(wa, ((0, 0), (0, S - hq // G), (0, 0))).astype(BF16)
    wm = w_out[hq:].astype(BF16)
    return pl.pallas_call(
        _dense_ln_b_body,
        grid=(M // tm,),
        in_specs=[pl.BlockSpec((G, tm, S), lambda i: (0, i, 0)),
                  pl.BlockSpec((tm, MEM_W), lambda i: (i, 0)),
                  pl.BlockSpec((G, S, D), lambda i: (0, 0, 0)),
                  pl.BlockSpec((MEM_W, D), lambda i: (0, 0)),
                  pl.BlockSpec((tm, D), lambda i: (i, 0)),
                  pl.BlockSpec((1, D), lambda i: (0, 0)),
                  pl.BlockSpec((1, D), lambda i: (0, 0))],
        out_specs=pl.BlockSpec((tm, D), lambda i: (i, 0)),
        out_shape=jax.ShapeDtypeStruct((M, D), F32),
        compiler_params=_params("parallel"),
        name="dense_post_norm_b",
    )(o_nsa, o_mem, wa, wm, x, g.reshape(1, D), b.reshape(1, D))


def _router_body(x_ref, w_ref, b_ref, o_ref):
    o_ref[...] = jnp.dot(x_ref[...], w_ref[...], preferred_element_type=F32,
                         precision=lax.Precision.HIGHEST) + b_ref[...]


def router_logits(x, w_r, b_r, tm=ROW_TILE):
    M, K = x.shape
    w = jnp.pad(w_r, ((0, 0), (0, LANES - N_EXPERTS)))
    b = jnp.pad(b_r, (0, LANES - N_EXPERTS)).reshape(1, LANES)
    out = pl.pallas_call(
        _router_body,
        grid=(M // tm,),
        in_specs=[pl.BlockSpec((tm, K), lambda i: (i, 0)),
                  pl.BlockSpec((K, LANES), lambda i: (0, 0)),
                  pl.BlockSpec((1, LANES), lambda i: (0, 0))],
        out_specs=pl.BlockSpec((tm, LANES), lambda i: (i, 0)),
        out_shape=jax.ShapeDtypeStruct((M, LANES), F32),
        compiler_params=_params("parallel"),
        name="router",
    )(x, w, b)
    return out[:, :N_EXPERTS]


def _experts_body(tile_ref, e_ref, lo_ref, hi_ref, first_ref, x_ref, wr_ref, wgu_ref, bgu_ref,
                  wdn_ref, bdn_ref, o_ref, wgu_bf, wdn_bf):
    i = pl.program_id(0)
    e = e_ref[i]
    e_prev = e_ref[jnp.maximum(i - 1, 0)]
    lo, hi = lo_ref[i], hi_ref[i]

    @pl.when((i == 0) | (e != e_prev))
    def _():
        wgu_bf[...] = wgu_ref[0].astype(BF16)
        wdn_bf[...] = wdn_ref[0].astype(BF16)

    @pl.when(hi > lo)
    def _():
        h = jnp.dot(x_ref[...].astype(BF16), wgu_bf[...], preferred_element_type=F32) + bgu_ref[0]
        gate = jnp.minimum(h[:, :D_EXPERT], SWIGLU_LIMIT)
        up = jnp.clip(h[:, D_EXPERT:], -SWIGLU_LIMIT, SWIGLU_LIMIT)
        act = (up + 1.0) * gate * jax.nn.sigmoid(SWIGLU_ALPHA * gate)
        y = jnp.dot(act.astype(BF16), wdn_bf[...], preferred_element_type=F32) + bdn_ref[0]
        row = lax.broadcasted_iota(jnp.int32, (MOE_BLK, 1), 0)
        mine = (row >= lo) & (row < hi)
        y = y * wr_ref[...]

        @pl.when(first_ref[i] == 1)
        def _():
            o_ref[...] = jnp.where(mine, y, 0.0)

        @pl.when(first_ref[i] == 0)
        def _():
            o_ref[...] = jnp.where(mine, y, o_ref[...])


def experts(x_sorted, w_sorted, visits, w_gu, b_gu, w_dn, b_dn):
    R, D = x_sorted.shape
    n_vis = visits[0].shape[0]
    grid_spec = pltpu.PrefetchScalarGridSpec(
        num_scalar_prefetch=5,
        grid=(n_vis,),
        in_specs=[
            pl.BlockSpec((MOE_BLK, D), lambda i, t, e, lo, hi, f: (t[i], 0)),
            pl.BlockSpec((MOE_BLK, 1), lambda i, t, e, lo, hi, f: (t[i], 0)),
            pl.BlockSpec((1, D, 2 * D_EXPERT), lambda i, t, e, lo, hi, f: (e[i], 0, 0)),
            pl.BlockSpec((1, 1, 2 * D_EXPERT), lambda i, t, e, lo, hi, f: (e[i], 0, 0)),
            pl.BlockSpec((1, D_EXPERT, D), lambda i, t, e, lo, hi, f: (e[i], 0, 0)),
            pl.BlockSpec((1, 1, D), lambda i, t, e, lo, hi, f: (e[i], 0, 0)),
        ],
        out_specs=pl.BlockSpec((MOE_BLK, D), lambda i, t, e, lo, hi, f: (t[i], 0)),
        scratch_shapes=[pltpu.VMEM((D, 2 * D_EXPERT), BF16), pltpu.VMEM((D_EXPERT, D), BF16)],
    )
    return pl.pallas_call(
        _experts_body,
        grid_spec=grid_spec,
        out_shape=jax.ShapeDtypeStruct((R, D), F32),
        compiler_params=_params("arbitrary"),
        name="experts",
    )(*visits, x_sorted, w_sorted, w_gu, b_gu.reshape(N_EXPERTS, 1, -1), w_dn,
      b_dn.reshape(N_EXPERTS, 1, -1))


def _combine_ln_body(y_ref, x_ref, g_ref, b_ref, o_ref):
    D = x_ref.shape[1]
    y = y_ref[:, 0:D]
    for k in range(1, TOP_K):
        y = y + y_ref[:, k * D:(k + 1) * D]
    o_ref[...] = _layer_norm_rows(DN_ALPHA * x_ref[...] + y, g_ref[...], b_ref[...])


def combine_post_norm(y_assign, x, g, b, tm=ROW_TILE):
    M, D = x.shape
    return pl.pallas_call(
        _combine_ln_body,
        grid=(M // tm,),
        in_specs=[pl.BlockSpec((tm, TOP_K * D), lambda i: (i, 0)),
                  pl.BlockSpec((tm, D), lambda i: (i, 0)),
                  pl.BlockSpec((1, D), lambda i: (0, 0)),
                  pl.BlockSpec((1, D), lambda i: (0, 0))],
        out_specs=pl.BlockSpec((tm, D), lambda i: (i, 0)),
        out_shape=jax.ShapeDtypeStruct((M, D), F32),
        compiler_params=_params("parallel"),
        name="combine_post_norm",
    )(y_assign, x, g.reshape(1, D), b.reshape(1, D))


def expert_visits(starts, ends, n_tiles):
    n_vis = n_tiles + N_EXPERTS
    sizes = ends - starts
    first_tile = starts // MOE_BLK
    n_e = jnp.where(sizes > 0, (ends - 1) // MOE_BLK - first_tile + 1, 0)
    v_end = jnp.cumsum(n_e)
    v_start = v_end - n_e
    total = v_end[-1]
    v = jnp.arange(n_vis)
    real = v < total
    vc = jnp.minimum(v, total - 1)
    e_v = jnp.minimum(jnp.searchsorted(v_end, vc, side='right'), N_EXPERTS - 1)
    tile_v = first_tile[e_v] + vc - v_start[e_v]
    lo = jnp.clip(starts[e_v] - tile_v * MOE_BLK, 0, MOE_BLK)
    hi = jnp.where(real, jnp.clip(ends[e_v] - tile_v * MOE_BLK, 0, MOE_BLK), lo)
    first = real & ((v == 0) | (tile_v != jnp.roll(tile_v, 1)))
    return tuple(a.astype(jnp.int32) for a in (tile_v, e_v, lo, hi, first))


def moe_post_norm(x, w_r, b_r, w_gu, b_gu, w_dn, b_dn, g, b):
    N, D = x.shape
    logits = router_logits(x, w_r, b_r)
    top_v, top_e = lax.top_k(logits, TOP_K)
    wts = jax.nn.softmax(top_v, axis=-1)
    NK = N * TOP_K
    ids = jnp.arange(NK, dtype=jnp.int32)
    e_sorted, order, w_sorted = lax.sort(
        (top_e.reshape(-1).astype(jnp.int32), ids, wts.reshape(-1)), num_keys=1, is_stable=True)
    _, pos = lax.sort((order, ids), num_keys=1)
    ends = jnp.searchsorted(e_sorted, jnp.arange(N_EXPERTS, dtype=jnp.int32), side='right')
    starts = jnp.concatenate([jnp.zeros((1,), ends.dtype), ends[:-1]])
    visits = expert_visits(starts, ends, NK // MOE_BLK)
    y_sorted = experts(x[order // TOP_K], w_sorted.reshape(NK, 1), visits, w_gu, b_gu, w_dn, b_dn)
    return combine_post_norm(y_sorted[pos].reshape(N, TOP_K * D), x, g, b)


NSA_TQ = 256
NSA_TK = 512
NSA_TW = 256
HEADS_PER_KV = N_HEADS // N_KV
MASK_OFF = -(2.0 ** 30)
SLAB = 2 * LANES


def _flash_step(s, v, m_ref, l_ref, acc_ref):
    m_old = m_ref[...]
    m_new = jnp.maximum(m_old, jnp.max(s, axis=1, keepdims=True))
    alpha = jnp.exp(m_old - m_new)
    p = jnp.exp(s - m_new)
    l_ref[...] = alpha * l_ref[...] + jnp.sum(p, axis=1, keepdims=True)
    acc_ref[...] = alpha * acc_ref[...] + jnp.dot(p.astype(BF16), v, preferred_element_type=F32)
    m_ref[...] = m_new


def _nt_dot(a, b):
    return lax.dot_general(a, b, (((1,), (1,)), ((), ())), preferred_element_type=F32)


def _nsa_prompt_body(p_ref, ks_ref, kw_ref, kc_ref, covt_ref, o_ref,
                     qaug_ref, m_ref, l_ref, acc_ref, *, n_cmp, n_sel):
    TQ, R = NSA_TQ, HEADS_PER_KV
    i = pl.program_id(2)
    q0 = i * TQ
    slab = p_ref[...]
    a, bv = slab[:, :LANES], slab[:, LANES:]
    lane = lax.broadcasted_iota(jnp.int32, (TQ, LANES), 1)
    lo = lane < HEAD_DIM
    scale = HEAD_DIM ** -0.5
    q_heads = (jnp.where(lo, a, 0.0), jnp.where(lo, pltpu.roll(a, HEAD_DIM, 1), 0.0),
               jnp.where(lo, bv, 0.0))
    q3 = (jnp.concatenate(q_heads, axis=0) * scale).astype(BF16)
    row_q = q0 + lax.broadcasted_iota(jnp.int32, (TQ, 1), 0)
    row_q3 = jnp.concatenate([row_q] * R, axis=0)

    kc = kc_ref[0, 0]
    nc = kc.shape[0]
    s = _nt_dot(q3, kc)
    n_idx = lax.broadcasted_iota(jnp.int32, (R * TQ, nc), 1)
    vis = (n_idx * CMP_STRIDE + (CMP_LEN - 1) <= row_q3) & (n_idx < n_cmp)
    s = jnp.where(vis, s, MASK_OFF)
    m = jnp.max(s, axis=1, keepdims=True)
    p = jnp.where(vis, jnp.exp(s - m), 0.0)
    d = jnp.sum(p, axis=1, keepdims=True)
    p_c = p / jnp.where(d > 0, d, 1.0)
    o_c = jnp.dot(p_c.astype(BF16), kc, preferred_element_type=F32)

    p_sum = p_c[0:TQ]
    for r in range(1, R):
        p_sum = p_sum + p_c[r * TQ:(r + 1) * TQ]
    p_hi = p_sum.astype(BF16)
    p_lo = (p_sum - p_hi.astype(F32)).astype(BF16)
    covt = covt_ref[...]
    imp = _nt_dot(covt, p_hi) + _nt_dot(covt, p_lo)
    j = lax.broadcasted_iota(jnp.int32, (n_sel, TQ), 0)
    cur = jnp.right_shift(q0 + lax.broadcasted_iota(jnp.int32, (n_sel, TQ), 1),
                          SEL_BLK.bit_length() - 1)
    forced = (j == 0) | (j == cur) | (j == cur - 1)
    imp = jnp.where(forced, jnp.inf, jnp.where(j > cur, -jnp.inf, imp))
    rank = jnp.zeros((n_sel, TQ), F32)
    for b in range(n_sel):
        row = imp[b:b + 1, :]
        ahead = (row > imp) | ((row == imp) & (j > b))
        rank = rank + jnp.where(ahead, 1.0, 0.0)
    sel_off = jnp.where(rank < float(min(N_SEL, n_sel)), 0.0, MASK_OFF)
    sel_off = jnp.concatenate([sel_off, jnp.zeros((LANES - n_sel, TQ), F32)], axis=0)
    mask_q = sel_off.T.astype(BF16)
    qaug_ref[:, :LANES] = q3
    qaug_ref[:, LANES:] = jnp.concatenate([mask_q] * R, axis=0)

    def reset():
        m_ref[...] = jnp.full(m_ref.shape, MASK_OFF, F32)
        l_ref[...] = jnp.zeros(l_ref.shape, F32)
        acc_ref[...] = jnp.zeros(acc_ref.shape, F32)

    def finish():
        return acc_ref[...] / l_ref[...]

    reset()
    n_full = q0 // NSA_TK

    def sel_tile(t, masked):
        k0 = pl.multiple_of(t * NSA_TK, NSA_TK)
        kt = ks_ref[0, 0, pl.ds(k0, NSA_TK), :]
        s = _nt_dot(qaug_ref[...], kt)
        if masked:
            k_pos = k0 + lax.broadcasted_iota(jnp.int32, (R * TQ, NSA_TK), 1)
            s = jnp.where(k_pos <= row_q3, s, MASK_OFF)
        _flash_step(s, kt[:, :LANES], m_ref, l_ref, acc_ref)

    def sel_loop(t, c):
        sel_tile(t, False)
        return c

    lax.fori_loop(0, n_full, sel_loop, 0)
    sel_tile(n_full, True)
    o_s = finish()

    reset()
    w0 = jnp.maximum(q0 - WINDOW, 0)
    n_w = (q0 + TQ - w0) // NSA_TW

    def win_loop(u, c):
        k0 = pl.multiple_of(w0 + u * NSA_TW, NSA_TW)
        kt = kw_ref[0, 0, pl.ds(k0, NSA_TW), :]
        s = _nt_dot(q3, kt)
        dist = row_q3 - (k0 + lax.broadcasted_iota(jnp.int32, (R * TQ, NSA_TW), 1))
        s = jnp.where((dist >= 0) & (dist <= WINDOW), s, MASK_OFF)
        _flash_step(s, kt, m_ref, l_ref, acc_ref)
        return c

    lax.fori_loop(0, n_w, win_loop, 0)
    o_w = finish()

    gates = jax.nn.sigmoid(bv)
    outs = []
    for r in range(R):
        c0 = HEAD_DIM + 3 * r
        rows = slice(r * TQ, (r + 1) * TQ)
        outs.append(gates[:, c0:c0 + 1] * o_c[rows] + gates[:, c0 + 1:c0 + 2] * o_s[rows]
                    + gates[:, c0 + 2:c0 + 3] * o_w[rows])
    o_ref[0, :, :LANES] = jnp.where(lo, pltpu.roll(outs[0], HEAD_DIM, 1), outs[1])
    o_ref[0, :, LANES:] = jnp.where(lo, pltpu.roll(outs[2], HEAD_DIM, 1), 0.0)


def nsa_prompt_pallas(p, ks, kw, kc, covt, B, T, n_cmp):
    n_sel = covt.shape[0]
    nq = T // NSA_TQ
    rows = HEADS_PER_KV * NSA_TQ
    body = functools.partial(_nsa_prompt_body, n_cmp=n_cmp, n_sel=n_sel)
    return pl.pallas_call(
        body,
        grid=(B, N_KV, nq),
        in_specs=[pl.BlockSpec((NSA_TQ, SLAB), lambda b, g, i: (b * nq + i, g)),
                  pl.BlockSpec((1, 1, T, SLAB), lambda b, g, i: (b, g, 0, 0)),
                  pl.BlockSpec((1, 1, T, LANES), lambda b, g, i: (b, g, 0, 0)),
                  pl.BlockSpec((1, 1) + kc.shape[2:], lambda b, g, i: (b, g, 0, 0)),
                  pl.BlockSpec(covt.shape, lambda b, g, i: (0, 0))],
        out_specs=pl.BlockSpec((1, NSA_TQ, SLAB), lambda b, g, i: (g, b * nq + i, 0)),
        out_shape=jax.ShapeDtypeStruct((N_KV, B * T, SLAB), F32),
        scratch_shapes=[pltpu.VMEM((rows, SLAB), BF16), pltpu.VMEM((rows, 1), F32),
                        pltpu.VMEM((rows, 1), F32), pltpu.VMEM((rows, LANES), F32)],
        compiler_params=_params("parallel", "parallel", "arbitrary"),
        name="nsa_prompt",
    )(p, ks, kw, kc, covt)


def nsa_prompt_operands(kv_p, kc_all):
    B, T = kv_p.shape[:2]
    n_sel = T // SEL_BLK
    n_chunk = T // CMP_STRIDE
    n_cmp = n_chunk - CMP_R + 1
    kvt = kv_p.transpose(2, 0, 3, 1, 4).astype(BF16)
    e_t = (jnp.arange(T)[:, None] // SEL_BLK == jnp.arange(LANES)[None, :]).astype(BF16)
    e_t = jnp.broadcast_to(e_t, (B, N_KV, T, LANES))
    ks = jnp.concatenate([kvt[2], kvt[3], e_t], axis=-1)
    kw = jnp.concatenate([kvt[4], kvt[5]], axis=-1)
    kcv = kc_all.reshape(2, B, n_chunk, N_KV, HEAD_DIM).transpose(1, 3, 2, 0, 4)
    kcv = kcv.reshape(B, N_KV, n_chunk, 2 * HEAD_DIM).astype(BF16)
    c_start = jnp.arange(n_chunk) * CMP_STRIDE
    s_start = jnp.arange(n_sel) * SEL_BLK
    covt = ((c_start[None, :] < s_start[:, None] + SEL_BLK)
            & (c_start[None, :] + CMP_LEN - 1 >= s_start[:, None])
            & (jnp.arange(n_chunk)[None, :] < n_cmp)).astype(BF16)
    return ks, kw, kcv, covt


CMP_R = CMP_LEN // CMP_STRIDE
KV_W = N_KV * HEAD_DIM


def _compress_body(xa_ref, xb_ref, w1_ref, pe_ref, b1_ref, w2_ref, b2_ref, o_ref):
    m = o_ref.shape[1]
    acc = [None] * CMP_R
    for s in range(CMP_STRIDE):
        rows_s = pl.ds(s, m, stride=CMP_STRIDE)
        xs = jnp.concatenate([xa_ref[rows_s, :], xb_ref[rows_s, :]], axis=1)
        for r in range(CMP_R):
            t = jnp.dot((xs + pe_ref[0, r, s:s + 1, :]).astype(BF16), w1_ref[0, r, s],
                        preferred_element_type=F32)
            acc[r] = t if acc[r] is None else acc[r] + t
    hid = b1_ref[0] + acc[0]
    for r in range(1, CMP_R):
        hid = hid + pltpu.roll(acc[r], m - r, 0)
    o_ref[0] = jnp.dot(jax.nn.gelu(hid).astype(BF16), w2_ref[0],
                       preferred_element_type=F32) + b2_ref[0]


def compress_operands(cmp_pe, cmp_w1, cmp_b1, cmp_w2, cmp_b2):
    eye = jnp.eye(N_KV, dtype=F32)
    w1 = cmp_w1.reshape(2, CMP_R, CMP_STRIDE, HEAD_DIM, CMP_HID)
    w1 = jnp.einsum('krsdh,gG->krsgdGh', w1, eye).reshape(
        2, CMP_R, CMP_STRIDE, KV_W, N_KV * CMP_HID).astype(BF16)
    pe = jnp.tile(cmp_pe.reshape(2, CMP_R, CMP_STRIDE, HEAD_DIM), (1, 1, 1, N_KV))
    b1 = jnp.tile(cmp_b1, (1, N_KV)).reshape(2, 1, N_KV * CMP_HID)
    w2 = jnp.einsum('khd,gG->kghGd', cmp_w2, eye).reshape(2, N_KV * CMP_HID, KV_W).astype(BF16)
    b2 = jnp.tile(cmp_b2, (1, N_KV)).reshape(2, 1, KV_W)
    return w1, pe, b1, w2, b2


def compress(x2d, n_seq, seq_rows, seqs_per_step, ops):
    w1, pe, b1, w2, b2 = ops
    rows = seqs_per_step * seq_rows
    m = rows // CMP_STRIDE
    return pl.pallas_call(
        _compress_body,
        grid=(2, n_seq // seqs_per_step),
        in_specs=[pl.BlockSpec((rows, LANES), lambda k, i: (i, 2 * k)),
                  pl.BlockSpec((rows, LANES), lambda k, i: (i, 2 * k + 1)),
                  pl.BlockSpec((1,) + w1.shape[1:], lambda k, i: (k, 0, 0, 0, 0)),
                  pl.BlockSpec((1,) + pe.shape[1:], lambda k, i: (k, 0, 0, 0)),
                  pl.BlockSpec((1,) + b1.shape[1:], lambda k, i: (k, 0, 0)),
                  pl.BlockSpec((1,) + w2.shape[1:], lambda k, i: (k, 0, 0)),
                  pl.BlockSpec((1,) + b2.shape[1:], lambda k, i: (k, 0, 0))],
        out_specs=pl.BlockSpec((1, m, KV_W), lambda k, i: (k, i, 0)),
        out_shape=jax.ShapeDtypeStruct((2, n_seq * seq_rows // CMP_STRIDE, KV_W), F32),
        compiler_params=_params("arbitrary", "arbitrary"),
        name="compress",
    )(x2d, x2d, w1, pe, b1, w2, b2)


def _softmax_two(s_a, s_b):
    m = jnp.maximum(jnp.max(s_a, axis=1, keepdims=True), jnp.max(s_b, axis=1, keepdims=True))
    p_a, p_b = jnp.exp(s_a - m), jnp.exp(s_b - m)
    d = jnp.sum(p_a, axis=1, keepdims=True) + jnp.sum(p_b, axis=1, keepdims=True)
    return p_a, p_b, d


def _nsa_sample_body(p_ref, past_ref, new_ref, win_ref, kc_ref, cov_ref, esel_ref, o_ref,
                     *, past_len, n_cmp, n_sel):
    S = p_ref.shape[0]
    G, R = N_KV, HEADS_PER_KV
    w_buf = win_ref.shape[0]
    slab = p_ref[...]
    lane = lax.broadcasted_iota(jnp.int32, (S, LANES), 1)
    lo = lane < HEAD_DIM
    zero = jnp.zeros((S, LANES), F32)

    def half(x, src_hi, dst_hi):
        x = x if src_hi == dst_hi else pltpu.roll(x, HEAD_DIM, 1)
        return jnp.where(lo != dst_hi, x, 0.0)

    q_rows, gate_cols = [], ([], [], [])
    for g in range(G):
        a = slab[:, g * SLAB:g * SLAB + LANES]
        bv = slab[:, g * SLAB + LANES:(g + 1) * SLAB]
        sig = jax.nn.sigmoid(bv)
        for r in range(R):
            piece = half(a if r < 2 else bv, r == 1, g % 2 == 1)
            tiles = [zero, zero]
            tiles[g // 2] = piece
            q_rows.append(jnp.concatenate(tiles, axis=1))
            for br in range(3):
                c = HEAD_DIM + 3 * r + br
                gate_cols[br].append(sig[:, c:c + 1])
    q_all = (jnp.concatenate(q_rows, axis=0) * HEAD_DIM ** -0.5).astype(BF16)
    gate = [jnp.concatenate(cols, axis=0) for cols in gate_cols]
    n_rows = G * R * S
    t_col = lax.broadcasted_iota(jnp.int32, (S, 1), 0)
    t_row = jnp.concatenate([t_col] * (G * R), axis=0)
    q_pos = past_len + t_row

    kck = kc_ref[0].astype(BF16)
    kcv = kc_ref[1].astype(BF16)
    nc = kck.shape[0]
    s = _nt_dot(q_all, kck)
    n_idx = lax.broadcasted_iota(jnp.int32, (n_rows, nc), 1)
    vis = (n_idx * CMP_STRIDE + (CMP_LEN - 1) <= q_pos) & (n_idx < n_cmp)
    s = jnp.where(vis, s, MASK_OFF)
    m = jnp.max(s, axis=1, keepdims=True)
    p = jnp.where(vis, jnp.exp(s - m), 0.0)
    d = jnp.sum(p, axis=1, keepdims=True)
    p_c = p / jnp.where(d > 0, d, 1.0)
    o_c = jnp.dot(p_c.astype(BF16), kcv, preferred_element_type=F32)

    p_sum = []
    for g in range(G):
        acc = p_c[g * R * S:(g * R + 1) * S]
        for r in range(1, R):
            acc = acc + p_c[(g * R + r) * S:(g * R + r + 1) * S]
        p_sum.append(acc)
    p_sum = jnp.concatenate(p_sum, axis=0)
    p_hi = p_sum.astype(BF16)
    p_lo = (p_sum - p_hi.astype(F32)).astype(BF16)
    cov = cov_ref[...]
    imp = (jnp.dot(p_hi, cov, preferred_element_type=F32)
           + jnp.dot(p_lo, cov, preferred_element_type=F32))
    j = lax.broadcasted_iota(jnp.int32, (G * S, LANES), 1)
    cur = jnp.right_shift(past_len + jnp.concatenate([t_col] * G, axis=0), SEL_BLK.bit_length() - 1)
    forced = (j == 0) | (j == cur) | (j == cur - 1)
    imp = jnp.where(forced, jnp.inf, jnp.where((j > cur) | (j >= n_sel), -jnp.inf, imp))
    rank = jnp.zeros((G * S, LANES), F32)
    for b in range(n_sel):
        col = imp[:, b:b + 1]
        rank = rank + jnp.where((col > imp) | ((col == imp) & (j > b)), 1.0, 0.0)
    sel_off = jnp.where(rank < float(min(N_SEL, n_sel)), 0.0, MASK_OFF)
    mask_rows = jnp.concatenate(
        [sel_off[g * S:(g + 1) * S] for g in range(G) for _ in range(R)], axis=0).astype(BF16)

    new = new_ref[...]
    pad = jnp.zeros((LANES - S, KV_W), F32)
    u = lax.broadcasted_iota(jnp.int32, (n_rows, LANES), 1)
    new_vis = u <= t_row

    def new_keys(slot):
        return jnp.concatenate([new[:, slot * KV_W:(slot + 1) * KV_W], pad], axis=0).astype(BF16)

    esel = esel_ref[...]
    s_p = (_nt_dot(q_all, past_ref[:, :KV_W].astype(BF16))
           + jnp.dot(mask_rows, esel[:, :past_len], preferred_element_type=F32))
    s_n = (_nt_dot(q_all, new_keys(2))
           + jnp.dot(mask_rows, esel[:, past_len:], preferred_element_type=F32))
    s_n = jnp.where(new_vis, s_n, MASK_OFF)
    p_p, p_n, d = _softmax_two(s_p, s_n)
    o_s = (jnp.dot(p_p.astype(BF16), past_ref[:, KV_W:].astype(BF16), preferred_element_type=F32)
           + jnp.dot(p_n.astype(BF16), new_keys(3), preferred_element_type=F32)) / d

    s_w = _nt_dot(q_all, win_ref[:, :KV_W].astype(BF16))
    w_pos = (past_len - w_buf) + lax.broadcasted_iota(jnp.int32, (n_rows, w_buf), 1)
    dist = q_pos - w_pos
    s_w = jnp.where((w_pos >= 0) & (dist >= 0) & (dist <= WINDOW), s_w, MASK_OFF)
    s_wn = jnp.where(new_vis, _nt_dot(q_all, new_keys(4)), MASK_OFF)
    p_w, p_wn, d = _softmax_two(s_w, s_wn)
    o_w = (jnp.dot(p_w.astype(BF16), win_ref[:, KV_W:].astype(BF16), preferred_element_type=F32)
           + jnp.dot(p_wn.astype(BF16), new_keys(5), preferred_element_type=F32)) / d

    out = gate[0] * o_c + gate[1] * o_s + gate[2] * o_w

    for g in range(G):
        def head(r, dst_hi):
            rows = out[(g * R + r) * S:(g * R + r + 1) * S]
            return half(rows[:, (g // 2) * LANES:(g // 2 + 1) * LANES], g % 2 == 1, dst_hi)
        o_ref[:, g * SLAB:g * SLAB + LANES] = head(0, False) + head(1, True)
        o_ref[:, g * SLAB + LANES:(g + 1) * SLAB] = head(2, False)


def nsa_sample_pallas(p, row0, past2d, kv, cache_win2d, kc_all, Bs, S, past_len):
    w_buf = cache_win2d.shape[0] // Bs
    T = past_len + S
    n_sel = -(-T // SEL_BLK)
    nc = kc_all.shape[1] // Bs
    n_cmp = T // CMP_STRIDE - CMP_R + 1
    c_start = jnp.arange(nc) * CMP_STRIDE
    s_start = jnp.arange(LANES) * SEL_BLK
    cov = ((c_start[:, None] < s_start[None, :] + SEL_BLK)
           & (c_start[:, None] + CMP_LEN - 1 >= s_start[None, :])
           & (jnp.arange(nc)[:, None] < n_cmp) & (jnp.arange(LANES)[None, :] < n_sel)).astype(BF16)
    k_pos = jnp.arange(past_len + LANES)
    esel = ((k_pos[None, :] // SEL_BLK == jnp.arange(LANES)[:, None])
            & (k_pos[None, :] < T)).astype(BF16)
    body = functools.partial(_nsa_sample_body, past_len=past_len, n_cmp=n_cmp, n_sel=n_sel)
    blk0 = row0 // S
    return pl.pallas_call(
        body,
        grid=(Bs,),
        in_specs=[pl.BlockSpec((S, N_KV * SLAB), lambda b: (blk0 + b, 0)),
                  pl.BlockSpec((past_len, 2 * KV_W), lambda b: (b, 1)),
                  pl.BlockSpec((S, 6 * KV_W), lambda b: (blk0 + b, 0)),
                  pl.BlockSpec((w_buf, 2 * KV_W), lambda b: (b, 0)),
                  pl.BlockSpec((2, nc, KV_W), lambda b: (0, b, 0)),
                  pl.BlockSpec(cov.shape, lambda b: (0, 0)),
                  pl.BlockSpec(esel.shape, lambda b: (0, 0))],
        out_specs=pl.BlockSpec((S, N_KV * SLAB), lambda b: (b, 0)),
        out_shape=jax.ShapeDtypeStruct((Bs * S, N_KV * SLAB), F32),
        compiler_params=_params("parallel"),
        name="nsa_sample",
    )(p, past2d, kv, cache_win2d, kc_all, cov, esel)


RG_TB = 256


def _rglru_body(*refs, seg, carry):
    if carry:
        (gate_ref, u_ref, cw_ref, cb_ref, wai_ref, bai_ref, sp_ref,
         y_ref, hl_ref, prev_ref, h_ref) = refs
    else:
        (gate_ref, u_ref, a1_ref, a2_ref, a3_ref, h0_ref, cw_ref, cb_ref, wai_ref, bai_ref,
         sp_ref, y_ref, hall_ref) = refs
        alt_refs = (a1_ref, a2_ref, a3_ref)
    TB = u_ref.shape[0]
    u = u_ref[...]
    t = lax.broadcasted_iota(jnp.int32, (TB, 1), 0) & (seg - 1)
    if carry:
        @pl.when(pl.program_id(1) == 0)
        def _():
            prev_ref[...] = jnp.zeros(prev_ref.shape, F32)
            h_ref[...] = jnp.zeros(h_ref.shape, F32)
        prev = prev_ref[...]

    xc = u * cw_ref[CONV_W - 1:CONV_W, :] + cb_ref[...]
    for k in range(1, CONV_W):
        back = pltpu.roll(prev, k, 0) if carry else alt_refs[k - 1][...]
        xc = xc + jnp.where(t >= k, pltpu.roll(u, k, 0), back) * cw_ref[CONV_W - 1 - k:CONV_W - k, :]

    ri = jnp.dot(xc.astype(BF16), wai_ref[...], preferred_element_type=F32) + bai_ref[...]
    r = jax.nn.sigmoid(ri[:, :D_RNN])
    i_gate = jax.nn.sigmoid(ri[:, D_RNN:])
    log_a = -RG_C * r * sp_ref[...]
    a = jnp.exp(log_a)
    x2 = 2.0 * log_a
    e2 = jnp.exp(x2)
    one = e2 == 1.0
    expm1 = jnp.where(one, x2, (e2 - 1.0) * x2 / jnp.where(one, 1.0, jnp.log(e2)))
    b = jnp.sqrt(-expm1) * (i_gate * xc)

    d = 1
    while d < seg:
        ok = t >= d
        b = jnp.where(ok, a * pltpu.roll(b, d, 0) + b, b)
        a = jnp.where(ok, a * pltpu.roll(a, d, 0), a)
        d *= 2
    h = a * (h_ref[0:1, :] if carry else h0_ref[...]) + b
    y_ref[...] = h * jax.nn.gelu(gate_ref[...])
    if carry:
        prev_ref[...] = u
        h_last = h[TB - 1:TB, :]
        h_ref[...] = jnp.broadcast_to(h_last, h_ref.shape)
        hl_ref[0] = h_last
    else:
        hall_ref[...] = h


def rglru_operands(conv_w, conv_b, w_a, b_a, w_i, b_i, lam):
    eye = jnp.eye(N_RG_BLOCKS, dtype=F32)
    bd = lambda w: jnp.einsum('ncd,nm->ncmd', w, eye).reshape(D_RNN, D_RNN)
    wai = jnp.concatenate([bd(w_a), bd(w_i)], axis=1).astype(BF16)
    bai = jnp.concatenate([b_a, b_i]).reshape(1, 2 * D_RNN)
    sp = jax.nn.softplus(-lam.astype(F32)).reshape(1, D_RNN)
    return conv_w, conv_b.reshape(1, D_RNN), wai, bai, sp


def _rg_param_specs(n_grid):
    z = (lambda *_: (0, 0))
    return [pl.BlockSpec((CONV_W, D_RNN), z), pl.BlockSpec((1, D_RNN), z),
            pl.BlockSpec((D_RNN, 2 * D_RNN), z), pl.BlockSpec((1, 2 * D_RNN), z),
            pl.BlockSpec((1, D_RNN), z)]


def rglru_prompt(p, B, T, ops):
    nblk = T // RG_TB
    body = functools.partial(_rglru_body, seg=RG_TB, carry=True)
    return pl.pallas_call(
        body,
        grid=(B, nblk),
        in_specs=[pl.BlockSpec((RG_TB, D_RNN), lambda b, i: (b * nblk + i, 0)),
                  pl.BlockSpec((RG_TB, D_RNN), lambda b, i: (b * nblk + i, 1))] + _rg_param_specs(2),
        out_specs=[pl.BlockSpec((RG_TB, D_RNN), lambda b, i: (b * nblk + i, 0)),
                   pl.BlockSpec((1, 1, D_RNN), lambda b, i: (b, 0, 0))],
        out_shape=[jax.ShapeDtypeStruct((B * T, D_RNN), F32),
                   jax.ShapeDtypeStruct((B, 1, D_RNN), F32)],
        scratch_shapes=[pltpu.VMEM((RG_TB, D_RNN), F32), pltpu.VMEM((8, D_RNN), F32)],
        compiler_params=_params("parallel", "arbitrary"),
        name="rglru_prompt",
    )(p, p, *ops)


def rglru_sample(p, row0, B, S, conv_buf, h0, ops, tb=512):
    ext = jnp.concatenate([conv_buf, jnp.zeros((B, S, D_RNN), F32)], axis=1)
    alts = [ext[:, CONV_W - 1 - k:CONV_W - 1 - k + S].reshape(B * S, D_RNN)
            for k in range(1, CONV_W)]
    h0x = jnp.repeat(h0, S, axis=0)
    blk0 = row0 // tb
    body = functools.partial(_rglru_body, seg=S, carry=False)
    row = lambda i: (i, 0)
    return pl.pallas_call(
        body,
        grid=(B * S // tb,),
        in_specs=[pl.BlockSpec((tb, D_RNN), lambda i: (blk0 + i, 0)),
                  pl.BlockSpec((tb, D_RNN), lambda i: (blk0 + i, 1))]
                 + [pl.BlockSpec((tb, D_RNN), row)] * 4 + _rg_param_specs(1),
        out_specs=[pl.BlockSpec((tb, D_RNN), row), pl.BlockSpec((tb, D_RNN), row)],
        out_shape=[jax.ShapeDtypeStruct((B * S, D_RNN), F32)] * 2,
        compiler_params=_params("parallel"),
        name="rglru_sample",
    )(p, p, *alts, h0x, *ops)


def _mem_attn_body(q_ref, kv_ref, o_ref):
    TB = q_ref.shape[0]
    q = q_ref[...] * MEM_HD ** -0.5
    head = jnp.right_shift(lax.broadcasted_iota(jnp.int32, (TB, MEM_W), 1), MEM_HD.bit_length() - 1)
    q_all = jnp.concatenate([jnp.where(head == h, q, 0.0) for h in range(MEM_HEADS)],
                            axis=0).astype(BF16)
    k = kv_ref[:, :MEM_W].astype(BF16)
    v = kv_ref[:, MEM_W:].astype(BF16)
    s = _nt_dot(q_all, k)
    p = jnp.exp(s - jnp.max(s, axis=1, keepdims=True))
    o = jnp.dot(p.astype(BF16), v, preferred_element_type=F32) / jnp.sum(p, axis=1, keepdims=True)
    out = jnp.where(head == 0, o[0:TB], 0.0)
    for h in range(1, MEM_HEADS):
        out = out + jnp.where(head == h, o[h * TB:(h + 1) * TB], 0.0)
    o_ref[...] = out


def mem_attend_pallas(p, row0, q_col_blk, n_seq, seq_rows, tb, kv2d, kv_blk0, kv_col_blk):
    nblk = seq_rows // tb
    blk0 = row0 // tb
    return pl.pallas_call(
        _mem_attn_body,
        grid=(n_seq, nblk),
        in_specs=[pl.BlockSpec((tb, MEM_W), lambda b, i: (blk0 + b * nblk + i, q_col_blk)),
                  pl.BlockSpec((N_MEM, 2 * MEM_W), lambda b, i: (kv_blk0 + b, kv_col_blk))],
        out_specs=pl.BlockSpec((tb, MEM_W), lambda b, i: (b * nblk + i, 0)),
        out_shape=jax.ShapeDtypeStruct((n_seq * seq_rows, MEM_W), F32),
        compiler_params=_params("parallel", "parallel"),
        name="mem_attend",
    )(p, kv2d)


def masked_softmax(s, mask):
    s = jnp.where(mask, s, -jnp.inf)
    m = jnp.max(s, axis=-1, keepdims=True)
    m = jnp.where(jnp.isfinite(m), m, 0.0)
    p = jnp.exp(s - m)
    d = jnp.sum(p, axis=-1, keepdims=True)
    return p / jnp.where(d > 0, d, 1.0)


def mem_attend(q, mem_kv):
    B, T, _ = q.shape
    qh = q.reshape(B, T, MEM_HEADS, MEM_HD)
    k, v = mem_kv[:, :, 0], mem_kv[:, :, 1]
    s = jnp.einsum('bthd,bmhd->bhtm', qh, k).astype(F32) * (MEM_HD ** -0.5)
    p = jax.nn.softmax(s, axis=-1)
    o = jnp.einsum('bhtm,bmhd->bthd', p.astype(v.dtype), v)
    return o.reshape(B, T, MEM_W)


def _lin_combine(c1, c2):
    a1, b1 = c1
    a2, b2 = c2
    return a1 * a2, a2 * b1 + b2


def rglru_branch(u, conv_buf, h0, conv_w, conv_b, w_a, b_a, w_i, b_i, lam):
    B, T, C = u.shape
    full = jnp.concatenate([conv_buf.astype(u.dtype), u], axis=1)
    xc = lax.conv_general_dilated(full, conv_w.astype(u.dtype)[:, None, :], (1,), 'VALID',
                                  dimension_numbers=('NWC', 'WIO', 'NWC'),
                                  feature_group_count=C) + conv_b
    xb = xc.reshape(B, T, N_RG_BLOCKS, RG_BW)
    r = jax.nn.sigmoid(jnp.einsum('btnc,ncd->btnd', xb, w_a).reshape(B, T, C) + b_a)
    i = jax.nn.sigmoid(jnp.einsum('btnc,ncd->btnd', xb, w_i).reshape(B, T, C) + b_i)
    log_a = -RG_C * r.astype(F32) * jax.nn.softplus(-lam.astype(F32))
    a = jnp.exp(log_a)
    b = jnp.sqrt(-jnp.expm1(2.0 * log_a)) * (i * xc).astype(F32)
    b = b.at[:, 0].add(a[:, 0] * h0.astype(F32))
    _, h = lax.associative_scan(_lin_combine, (a, b), axis=1)
    return h.astype(u.dtype), h[:, -1].astype(u.dtype), full[:, T:]


def compress_tokens(kseq, pe, w1, b1, w2, b2):
    B, T, G, Dh = kseq.shape
    R = CMP_LEN // CMP_STRIDE
    n_chunk = T // CMP_STRIDE
    n_cmp = n_chunk - R + 1
    ch = kseq[:, :n_chunk * CMP_STRIDE].reshape(B, n_chunk, CMP_STRIDE, G, Dh)
    pe_r = pe.reshape(R, CMP_STRIDE, Dh)
    w1_r = w1.reshape(R, CMP_STRIDE, Dh, CMP_HID)
    hid = b1
    for r in range(R):
        hid = hid + jnp.einsum('bnsgd,sdh->bngh', ch[:, r:r + n_cmp] + pe_r[r][:, None, :], w1_r[r])
    out = jnp.einsum('bngh,hd->bngd', jax.nn.gelu(hid), w2) + b2
    ends = jnp.arange(n_cmp) * CMP_STRIDE + (CMP_LEN - 1)
    return out, ends


def nsa_context(kv4, cmp_pe, cmp_w1, cmp_b1, cmp_w2, cmp_b2):
    kc, ends = compress_tokens(kv4[:, :, 0], cmp_pe[0], cmp_w1[0], cmp_b1[0], cmp_w2[0], cmp_b2[0])
    vc, _ = compress_tokens(kv4[:, :, 1], cmp_pe[1], cmp_w1[1], cmp_b1[1], cmp_w2[1], cmp_b2[1])
    B, T = kv4.shape[:2]
    n_sel = -(-T // SEL_BLK)
    sel = jnp.pad(kv4[:, :, 2:4], ((0, 0), (0, n_sel * SEL_BLK - T), (0, 0), (0, 0), (0, 0)))
    sel = sel.reshape(B, n_sel, SEL_BLK, 2, N_KV, HEAD_DIM).transpose(3, 0, 4, 1, 2, 5)
    return kc, vc, ends, sel[0], sel[1]


def nsa_attend(q, gates, q_pos, kc, vc, c_end, ksb, vsb, kw, vw, w_pos):
    B, Qb, H, Dh = q.shape
    G = kc.shape[2]
    R = H // G
    n_sel = ksb.shape[2]
    scale = Dh ** -0.5
    qg = q.reshape(B, Qb, G, R, Dh)
    s_c = jnp.einsum('bqgrd,bngd->bgrqn', qg, kc).astype(F32) * scale
    p_c = masked_softmax(s_c, c_end[None, :] <= q_pos[:, None])
    o_c = jnp.einsum('bgrqn,bngd->bqgrd', p_c.astype(vc.dtype), vc)
    c_start = c_end - (CMP_LEN - 1)
    s_start = jnp.arange(n_sel) * SEL_BLK
    cover = ((c_start[:, None] < s_start[None, :] + SEL_BLK) & (c_end[:, None] >= s_start[None, :])).astype(F32)
    imp = jnp.einsum('bgrqn,nj->bgqj', p_c, cover)
    cur = q_pos // SEL_BLK
    j = jnp.arange(n_sel)
    forced = (j[None, :] == 0) | (j[None, :] == cur[:, None]) | (j[None, :] == cur[:, None] - 1)
    future = j[None, :] > cur[:, None]
    imp = jnp.where(forced, jnp.inf, jnp.where(future, -jnp.inf, imp))
    _, idx = lax.top_k(imp, min(N_SEL, n_sel))
    gather = jax.vmap(jax.vmap(lambda blocks, ids: blocks[ids]))
    ks = gather(ksb, idx)
    vs = gather(vsb, idx)
    n_top = idx.shape[-1]
    k_pos = idx[..., None] * SEL_BLK + jnp.arange(SEL_BLK)
    m_s = (k_pos <= q_pos[:, None, None]).reshape(B, G, 1, Qb, n_top * SEL_BLK)
    s_s = jnp.einsum('bqgrd,bgqksd->bgrqks', qg, ks).astype(F32) * scale
    p_s = masked_softmax(s_s.reshape(B, G, R, Qb, n_top * SEL_BLK), m_s)
    o_s = jnp.einsum('bgrqm,bgqmd->bqgrd', p_s.astype(vs.dtype), vs.reshape(B, G, Qb, n_top * SEL_BLK, Dh))
    dist = q_pos[:, None] - w_pos[None, :]
    m_w = (w_pos[None, :] >= 0) & (dist >= 0) & (dist <= WINDOW)
    s_w = jnp.einsum('bqgrd,bngd->bgrqn', qg, kw).astype(F32) * scale
    p_w = masked_softmax(s_w, m_w)
    o_w = jnp.einsum('bgrqn,bngd->bqgrd', p_w.astype(vw.dtype), vw)
    g = gates.reshape(B, Qb, G, R, 3)
    out = g[..., 0:1] * o_c + g[..., 1:2] * o_s + g[..., 2:3] * o_w
    return out.reshape(B, Qb, H * Dh)


def nsa_prompt(q, gates, ctx, k_win, v_win):
    B, T, H, Dh = q.shape
    n_blk = T // Q_BLK
    zpad = jnp.zeros((B, WINDOW) + k_win.shape[2:], k_win.dtype)
    kw_pad = jnp.concatenate([zpad, k_win], axis=1)
    vw_pad = jnp.concatenate([zpad.astype(v_win.dtype), v_win], axis=1)
    qb = q.reshape(B, n_blk, Q_BLK, H, Dh).swapaxes(0, 1)
    gb = gates.reshape(B, n_blk, Q_BLK, H, 3).swapaxes(0, 1)
    band = jnp.arange(WINDOW + Q_BLK)

    def block(args):
        qi, gi, i = args
        start = i * Q_BLK
        kw = lax.dynamic_slice_in_dim(kw_pad, start, WINDOW + Q_BLK, axis=1)
        vw = lax.dynamic_slice_in_dim(vw_pad, start, WINDOW + Q_BLK, axis=1)
        return nsa_attend(qi, gi, start + jnp.arange(Q_BLK), *ctx, kw, vw, start - WINDOW + band)

    out = lax.map(block, (qb, gb, jnp.arange(n_blk)))
    return out.swapaxes(0, 1).reshape(B, T, H * Dh)


def nsa_sample(q, gates, ctx, k_win, v_win, pos0, w_first):
    B, S, H, Dh = q.shape
    w_pos = w_first + jnp.arange(k_win.shape[1])
    q_pos = pos0 + jnp.arange(S)

    def step(args):
        qi, gi, t = args
        return nsa_attend(qi[:, None], gi[:, None], t[None], *ctx, k_win, v_win, w_pos)

    out = lax.map(step, (q.swapaxes(0, 1), gates.swapaxes(0, 1), q_pos))
    return out[:, :, 0].swapaxes(0, 1)


def kernel(x_prompt, x_sample, cache_kv, cache_win, cache_mem_kv, state_rg_h, state_rg_conv,
           page_table, mem_prompt, ln_g, ln_b, w_in_a, conv_w, conv_b, w_rg_a, b_rg_a,
           w_rg_i, b_rg_i, rg_lambda, w_out_a, w_kv_shared, cmp_pe, cmp_w1, cmp_b1, cmp_w2,
           cmp_b2, w_in_b, w_out_b, w_mem_kv, w_router, b_router, w_gate_up, b_gate_up,
           w_down, b_down):
    hq = N_HEADS * HEAD_DIM
    Bp, Tp, D = x_prompt.shape
    Bs, Ss, _ = x_sample.shape
    Np, Ns = Bp * Tp, Bs * Ss
    past_len = page_table.shape[1] * PAGE_SIZE
    w_buf = cache_win.shape[1]
    cmp_params = (cmp_pe, cmp_w1, cmp_b1, cmp_w2, cmp_b2)

    def split(t):
        return t[:Np].reshape(Bp, Tp, -1), t[Np:].reshape(Bs, Ss, -1)

    def join(tp, ts):
        return jnp.concatenate([tp.reshape(Np, -1), ts.reshape(Ns, -1)], axis=0)

    w_mem_all = w_mem_kv.transpose(1, 0, 2).reshape(D, DEPTH * 2 * MEM_W).astype(BF16)
    mem_all = dense(mem_prompt.reshape(Bp * N_MEM, D), w_mem_all, tm=Bp * N_MEM // 2)
    mem_kv_prompt = mem_all.reshape(Bp, N_MEM, DEPTH, 2, MEM_HEADS, MEM_HD).transpose(2, 0, 1, 3, 4, 5)
    mem_kv_sample = cache_mem_kv.reshape(DEPTH * Bs * N_MEM, 2 * MEM_W)

    def mem_attention(p, q_col_blk, l):
        o_p = mem_attend_pallas(p, 0, q_col_blk, Bp, Tp, ROW_TILE, mem_all, 0, l)
        o_s = mem_attend_pallas(p, Np, q_col_blk, Bs, Ss, Ss, mem_kv_sample, l * Bs, 0)
        return jnp.concatenate([o_p, o_s], axis=0)

    x = join(x_prompt, x_sample)
    hs_p, hs_s, bufs_p, bufs_s = [], [], [], []
    for l in range(DEPTH):
        if l < N_A:
            p = dense(x, w_in_a[l].astype(BF16))
            rg_ops = rglru_operands(conv_w[l], conv_b[l], w_rg_a[l], b_rg_a[l], w_rg_i[l],
                                    b_rg_i[l], rg_lambda[l])
            y_p, h_last_p = rglru_prompt(p, Bp, Tp, rg_ops)
            y_s, h_all_s = rglru_sample(p, Np, Bs, Ss, state_rg_conv[l], state_rg_h[l], rg_ops)
            hs_p.append(h_last_p.reshape(Bp, D_RNN))
            hs_s.append(h_all_s.reshape(Bs, Ss, D_RNN)[:, -1])
            assert min(Tp, Ss) >= CONV_W - 1
            u_p, u_s = split(p[:, D_RNN:2 * D_RNN])
            bufs_p.append(u_p[:, Tp - (CONV_W - 1):])
            bufs_s.append(u_s[:, Ss - (CONV_W - 1):])
            o_mem = mem_attention(p, 2 * D_RNN // MEM_W, l)
            x = dense_post_norm(jnp.concatenate([y_p, y_s], axis=0), o_mem, w_out_a[l], x,
                                ln_g[l, 0], ln_b[l, 0])
        else:
            jb = l - N_A
            if l == N_A:
                kv = dense(x, w_kv_shared.astype(BF16))
                kv_p, kv_s = split(kv)
                kv_p = kv_p.reshape(Bp, Tp, 6, N_KV, HEAD_DIM)
                kv_s = kv_s.reshape(Bs, Ss, 6, N_KV, HEAD_DIM)
                cmp_ops = compress_operands(*cmp_params)
                kc_p = compress(kv, Bp, Tp, 1, cmp_ops)
                assert (past_len + Ss) // CMP_STRIDE * CMP_STRIDE == past_len
                past2d = cache_kv.reshape(-1, PAGE_SIZE, 4 * KV_W)[page_table].reshape(
                    Bs * past_len, 4 * KV_W)
                kc_s = compress(past2d, Bs, past_len, 2, cmp_ops)
                win_s = jnp.concatenate([cache_win, kv_s[:, :, 4:]], axis=1)
                ks, kw, kcv, covt = nsa_prompt_operands(kv_p, kc_p)
            hpg = hq // N_KV
            wq = w_in_b[jb][:, :hq].reshape(D, N_KV, hpg)
            wg = w_in_b[jb][:, hq:hq + 3 * N_HEADS].reshape(D, N_KV, 3 * HEADS_PER_KV)
            slabs = jnp.concatenate(
                [wq, wg, jnp.zeros((D, N_KV, SLAB - hpg - 3 * HEADS_PER_KV), F32)], axis=-1)
            w_in = jnp.concatenate([slabs.reshape(D, N_KV * SLAB), w_in_b[jb][:, hq + 3 * N_HEADS:]],
                                   axis=1).astype(BF16)
            p = dense(x, w_in)
            o_nsa_p = nsa_prompt_pallas(p, ks, kw, kcv, covt, Bp, Tp,
                                        Tp // CMP_STRIDE - CMP_R + 1)
            o_nsa_s = nsa_sample_pallas(p, Np, past2d, kv, cache_win.reshape(Bs * w_buf, 2 * KV_W),
                                        kc_s, Bs, Ss, past_len)
            o_nsa = jnp.concatenate(
                [o_nsa_p, o_nsa_s.reshape(Ns, N_KV, SLAB).transpose(1, 0, 2)], axis=1)
            o_mem = mem_attention(p, N_KV * SLAB // MEM_W, l)
            x = dense_post_norm_b(o_nsa, o_mem, w_out_b[jb], x, ln_g[l, 0], ln_b[l, 0])
        x = moe_post_norm(x, w_router[l], b_router[l], w_gate_up[l], b_gate_up[l], w_down[l],
                          b_down[l], ln_g[l, 1], ln_b[l, 1])

    y_prompt, y_sample = split(x)
    kv_rows_prompt = kv_p[:, :, :4]
    kv_rows_sample = kv_s[:, :, :4]
    win_prompt = kv_p[:, Tp - min(WINDOW, Tp):, 4:]
    win_sample = win_s[:, win_s.shape[1] - min(WINDOW, past_len + Ss):]
    return (y_prompt, y_sample, kv_rows_prompt, kv_rows_sample, win_prompt, win_sample,
            mem_kv_prompt, jnp.stack(hs_p), jnp.stack(hs_s), jnp.stack(bufs_p), jnp.stack(bufs_s))
```

```python
import functools

import jax
import jax.numpy as jnp
from jax import lax
from jax.experimental import pallas as pl
from jax.experimental.pallas import tpu as pltpu

D_MODEL = 1024
DEPTH = 4
PAGE_SIZE = 128
N_A = DEPTH // 2
D_RNN = 768
N_RG_BLOCKS = 8
RG_BW = D_RNN // N_RG_BLOCKS
CONV_W = 4
RG_C = 8.0
N_HEADS = 12
N_KV = 4
HEAD_DIM = 64
CMP_LEN = 32
CMP_STRIDE = 16
CMP_HID = 128
SEL_BLK = 64
N_SEL = 16
WINDOW = 512
Q_BLK = 64
N_MEM = 256
MEM_HEADS = 4
MEM_HD = 64
MEM_W = MEM_HEADS * MEM_HD
N_EXPERTS = 32
TOP_K = 4
D_EXPERT = 1024
SWIGLU_LIMIT = 7.0
SWIGLU_ALPHA = 1.702
DN_ALPHA = (2 * DEPTH) ** 0.25
LN_EPS = 1e-5

F32 = jnp.float32
BF16 = jnp.bfloat16

ROW_TILE = 512
MOE_BLK = 256
LANES = 128
VMEM_LIMIT = 56 * 1024 * 1024


def _params(*sem):
    return pltpu.CompilerParams(dimension_semantics=sem, vmem_limit_bytes=VMEM_LIMIT)


def _dense_body(x_ref, w_ref, o_ref):
    o_ref[...] = jnp.dot(x_ref[...].astype(BF16), w_ref[...],
                         preferred_element_type=F32).astype(o_ref.dtype)


def dense(x, w_bf16, tm=ROW_TILE, rows=None, out_dtype=F32):
    M, K = x.shape
    M = M if rows is None else rows
    N = w_bf16.shape[1]
    return pl.pallas_call(
        _dense_body,
        grid=(M // tm,),
        in_specs=[pl.BlockSpec((tm, K), lambda i: (i, 0)),
                  pl.BlockSpec((K, N), lambda i: (0, 0))],
        out_specs=pl.BlockSpec((tm, N), lambda i: (i, 0)),
        out_shape=jax.ShapeDtypeStruct((M, N), out_dtype),
        compiler_params=_params("parallel"),
        name="dense",
    )(x, w_bf16)


def _layer_norm_rows(z, g, b):
    mu = jnp.mean(z, axis=-1, keepdims=True)
    zc = z - mu
    var = jnp.mean(zc * zc, axis=-1, keepdims=True)
    return zc * lax.rsqrt(var + LN_EPS) * g + b


def _dense_ln_body(m1_ref, m2_ref, w1_ref, w2_ref, x_ref, g_ref, b_ref, o_ref):
    y = (jnp.dot(m1_ref[...].astype(BF16), w1_ref[...], preferred_element_type=F32)
         + jnp.dot(m2_ref[...].astype(BF16), w2_ref[...], preferred_element_type=F32))
    o_ref[...] = _layer_norm_rows(DN_ALPHA * x_ref[...] + y, g_ref[...], b_ref[...])


def dense_post_norm(m1, m2, w, x, g, b, tm=ROW_TILE):
    M, K1 = m1.shape
    K2 = m2.shape[1]
    D = w.shape[1]
    row = lambda i: (i, 0)
    fixed = lambda i: (0, 0)
    return pl.pallas_call(
        _dense_ln_body,
        grid=(M // tm,),
        in_specs=[pl.BlockSpec((tm, K1), row), pl.BlockSpec((tm, K2), row),
                  pl.BlockSpec((K1, D), fixed), pl.BlockSpec((K2, D), fixed),
                  pl.BlockSpec((tm, D), row), pl.BlockSpec((1, D), fixed),
                  pl.BlockSpec((1, D), fixed)],
        out_specs=pl.BlockSpec((tm, D), row),
        out_shape=jax.ShapeDtypeStruct((M, D), F32),
        compiler_params=_params("parallel"),
        name="dense_post_norm",
    )(m1, m2, w[:K1].astype(BF16), w[K1:].astype(BF16), x, g.reshape(1, D), b.reshape(1, D))


def _dense_ln_b_body(a_ref, m_ref, wa_ref, wm_ref, x_ref, g_ref, b_ref, o_ref):
    y = jnp.dot(m_ref[...].astype(BF16), wm_ref[...], preferred_element_type=F32)
    for grp in range(N_KV):
        y = y + jnp.dot(a_ref[grp].astype(BF16), wa_ref[grp], preferred_element_type=F32)
    o_ref[...] = _layer_norm_rows(DN_ALPHA * x_ref[...] + y, g_ref[...], b_ref[...])


def dense_post_norm_b(o_nsa, o_mem, w_out, x, g, b, tm=ROW_TILE):
    G, M, S = o_nsa.shape
    D = w_out.shape[1]
    hq = N_HEADS * HEAD_DIM
    wa = w_out[:hq].reshape(G, hq // G, D)
    wa = jnp.pad(wa, ((0, 0), (0, S - hq // G), (0, 0))).astype(BF16)
    wm = w_out[hq:].astype(BF16)
    return pl.pallas_call(
        _dense_ln_b_body,
        grid=(M // tm,),
        in_specs=[pl.BlockSpec((G, tm, S), lambda i: (0, i, 0)),
                  pl.BlockSpec((tm, MEM_W), lambda i: (i, 0)),
                  pl.BlockSpec((G, S, D), lambda i: (0, 0, 0)),
                  pl.BlockSpec((MEM_W, D), lambda i: (0, 0)),
                  pl.BlockSpec((tm, D), lambda i: (i, 0)),
                  pl.BlockSpec((1, D), lambda i: (0, 0)),
                  pl.BlockSpec((1, D), lambda i: (0, 0))],
        out_specs=pl.BlockSpec((tm, D), lambda i: (i, 0)),
        out_shape=jax.ShapeDtypeStruct((M, D), F32),
        compiler_params=_params("parallel"),
        name="dense_post_norm_b",
    )(o_nsa, o_mem, wa, wm, x, g.reshape(1, D), b.reshape(1, D))


def _router_body(x_ref, w_ref, b_ref, o_ref):
    o_ref[...] = jnp.dot(x_ref[...], w_ref[...], preferred_element_type=F32,
                         precision=lax.Precision.HIGHEST) + b_ref[...]


def router_logits(x, w_r, b_r, tm=ROW_TILE):
    M, K = x.shape
    w = jnp.pad(w_r, ((0, 0), (0, LANES - N_EXPERTS)))
    b = jnp.pad(b_r, (0, LANES - N_EXPERTS)).reshape(1, LANES)
    out = pl.pallas_call(
        _router_body,
        grid=(M // tm,),
        in_specs=[pl.BlockSpec((tm, K), lambda i: (i, 0)),
                  pl.BlockSpec((K, LANES), lambda i: (0, 0)),
                  pl.BlockSpec((1, LANES), lambda i: (0, 0))],
        out_specs=pl.BlockSpec((tm, LANES), lambda i: (i, 0)),
        out_shape=jax.ShapeDtypeStruct((M, LANES), F32),
        compiler_params=_params("parallel"),
        name="router",
    )(x, w, b)
    return out[:, :N_EXPERTS]


def _experts_body(tile_ref, e_ref, lo_ref, hi_ref, first_ref, x_ref, wr_ref, wgu_ref, bgu_ref,
                  wdn_ref, bdn_ref, o_ref, wgu_bf, wdn_bf):
    i = pl.program_id(0)
    e = e_ref[i]
    e_prev = e_ref[jnp.maximum(i - 1, 0)]
    lo, hi = lo_ref[i], hi_ref[i]

    @pl.when((i == 0) | (e != e_prev))
    def _():
        wgu_bf[...] = wgu_ref[0].astype(BF16)
        wdn_bf[...] = wdn_ref[0].astype(BF16)

    @pl.when(hi > lo)
    def _():
        h = jnp.dot(x_ref[...].astype(BF16), wgu_bf[...], preferred_element_type=F32) + bgu_ref[0]
        gate = jnp.minimum(h[:, :D_EXPERT], SWIGLU_LIMIT)
        up = jnp.clip(h[:, D_EXPERT:], -SWIGLU_LIMIT, SWIGLU_LIMIT)
        act = (up + 1.0) * gate * jax.nn.sigmoid(SWIGLU_ALPHA * gate)
        y = jnp.dot(act.astype(BF16), wdn_bf[...], preferred_element_type=F32) + bdn_ref[0]
        row = lax.broadcasted_iota(jnp.int32, (MOE_BLK, 1), 0)
        mine = (row >= lo) & (row < hi)
        y = y * wr_ref[...]

        @pl.when(first_ref[i] == 1)
        def _():
            o_ref[...] = jnp.where(mine, y, 0.0)

        @pl.when(first_ref[i] == 0)
        def _():
            o_ref[...] = jnp.where(mine, y, o_ref[...])


def experts(x_sorted, w_sorted, visits, layer, w_gu, b_gu, w_dn, b_dn):
    R, D = x_sorted.shape
    n_vis = visits[0].shape[0]
    e0 = layer * N_EXPERTS
    grid_spec = pltpu.PrefetchScalarGridSpec(
        num_scalar_prefetch=5,
        grid=(n_vis,),
        in_specs=[
            pl.BlockSpec((MOE_BLK, D), lambda i, t, e, lo, hi, f: (t[i], 0)),
            pl.BlockSpec((MOE_BLK, 1), lambda i, t, e, lo, hi, f: (t[i], 0)),
            pl.BlockSpec((1, D, 2 * D_EXPERT), lambda i, t, e, lo, hi, f: (e0 + e[i], 0, 0)),
            pl.BlockSpec((1, 1, 2 * D_EXPERT), lambda i, t, e, lo, hi, f: (e0 + e[i], 0, 0)),
            pl.BlockSpec((1, D_EXPERT, D), lambda i, t, e, lo, hi, f: (e0 + e[i], 0, 0)),
            pl.BlockSpec((1, 1, D), lambda i, t, e, lo, hi, f: (e0 + e[i], 0, 0)),
        ],
        out_specs=pl.BlockSpec((MOE_BLK, D), lambda i, t, e, lo, hi, f: (t[i], 0)),
        scratch_shapes=[pltpu.VMEM((D, 2 * D_EXPERT), BF16), pltpu.VMEM((D_EXPERT, D), BF16)],
    )
    return pl.pallas_call(
        _experts_body,
        grid_spec=grid_spec,
        out_shape=jax.ShapeDtypeStruct((R, D), F32),
        compiler_params=_params("arbitrary"),
        name="experts",
    )(*visits, x_sorted, w_sorted, w_gu.reshape((-1,) + w_gu.shape[-2:]),
      b_gu.reshape(-1, 1, b_gu.shape[-1]), w_dn.reshape((-1,) + w_dn.shape[-2:]),
      b_dn.reshape(-1, 1, b_dn.shape[-1]))


def _combine_ln_body(y_ref, x_ref, g_ref, b_ref, o_ref):
    y = y_ref[0]
    for k in range(1, TOP_K):
        y = y + y_ref[k]
    o_ref[...] = _layer_norm_rows(DN_ALPHA * x_ref[...] + y, g_ref[...], b_ref[...])


def combine_post_norm(y_assign, x, g, b, tm=ROW_TILE):
    M, D = x.shape
    return pl.pallas_call(
        _combine_ln_body,
        grid=(M // tm,),
        in_specs=[pl.BlockSpec((TOP_K, tm, D), lambda i: (0, i, 0)),
                  pl.BlockSpec((tm, D), lambda i: (i, 0)),
                  pl.BlockSpec((1, D), lambda i: (0, 0)),
                  pl.BlockSpec((1, D), lambda i: (0, 0))],
        out_specs=pl.BlockSpec((tm, D), lambda i: (i, 0)),
        out_shape=jax.ShapeDtypeStruct((M, D), F32),
        compiler_params=_params("parallel"),
        name="combine_post_norm",
    )(y_assign, x, g.reshape(1, D), b.reshape(1, D))


def expert_visits(starts, ends, n_tiles):
    n_vis = n_tiles + N_EXPERTS
    sizes = ends - starts
    first_tile = starts // MOE_BLK
    n_e = jnp.where(sizes > 0, (ends - 1) // MOE_BLK - first_tile + 1, 0)
    v_end = jnp.cumsum(n_e)
    v_start = v_end - n_e
    total = v_end[-1]
    v = jnp.arange(n_vis)
    real = v < total
    vc = jnp.minimum(v, total - 1)
    e_v = jnp.minimum(jnp.sum(v_end[None, :] <= vc[:, None], axis=1), N_EXPERTS - 1)
    tile_v = first_tile[e_v] + vc - v_start[e_v]
    lo = jnp.clip(starts[e_v] - tile_v * MOE_BLK, 0, MOE_BLK)
    hi = jnp.where(real, jnp.clip(ends[e_v] - tile_v * MOE_BLK, 0, MOE_BLK), lo)
    first = real & ((v == 0) | (tile_v != jnp.roll(tile_v, 1)))
    return tuple(a.astype(jnp.int32) for a in (tile_v, e_v, lo, hi, first))


def moe_post_norm(x, layer, w_r, b_r, w_gu, b_gu, w_dn, b_dn, g, b):
    N, D = x.shape
    logits = router_logits(x, w_r, b_r)
    top_v, top_e = lax.top_k(logits, TOP_K)
    wts = jax.nn.softmax(top_v, axis=-1)
    NK = N * TOP_K
    ids = jnp.arange(NK, dtype=jnp.int32)
    e_sorted, order, w_sorted = lax.sort(
        (top_e.reshape(-1).astype(jnp.int32), ids, wts.reshape(-1)), num_keys=1, is_stable=True)
    _, pos = lax.sort((order, ids), num_keys=1)
    ends = jnp.sum(e_sorted[None, :] <= jnp.arange(N_EXPERTS, dtype=jnp.int32)[:, None],
                   axis=1, dtype=jnp.int32)
    starts = jnp.concatenate([jnp.zeros((1,), ends.dtype), ends[:-1]])
    visits = expert_visits(starts, ends, NK // MOE_BLK)
    y_sorted = experts(x[order // TOP_K], w_sorted.reshape(NK, 1), visits, layer,
                       w_gu, b_gu, w_dn, b_dn)
    y_k = y_sorted[pos.reshape(N, TOP_K).T.reshape(-1)].reshape(TOP_K, N, D)
    return combine_post_norm(y_k, x, g, b)


NSA_TQ = 256
NSA_TK = 512
NSA_TW = 256
HEADS_PER_KV = N_HEADS // N_KV
MASK_OFF = -(2.0 ** 30)
SLAB = 2 * LANES


def _flash_step(s, v, rows, m_ref, l_ref, acc_ref):
    m_old = m_ref[rows, :]
    m_new = jnp.maximum(m_old, jnp.max(s, axis=1, keepdims=True))
    alpha = jnp.exp(m_old - m_new)
    p = jnp.exp(s - m_new)
    l_ref[rows, :] = alpha * l_ref[rows, :] + jnp.sum(p, axis=1, keepdims=True)
    acc_ref[rows, :] = alpha * acc_ref[rows, :] + jnp.dot(p.astype(BF16), v,
                                                          preferred_element_type=F32)
    m_ref[rows, :] = m_new


def _nt_dot(a, b):
    return lax.dot_general(a, b, (((1,), (1,)), ((), ())), preferred_element_type=F32)


def _nsa_prompt_body(p_ref, ks_ref, kw_ref, et_ref, kc_ref, covt_ref, o_ref,
                     qaug_ref, m_ref, l_ref, acc_ref, *, n_cmp, n_sel):
    TQ, R = NSA_TQ, HEADS_PER_KV
    i = pl.program_id(2)
    q0 = i * TQ
    slab = p_ref[...]
    a, bv = slab[:, :LANES], slab[:, LANES:]
    lane = lax.broadcasted_iota(jnp.int32, (TQ, LANES), 1)
    lo = lane < HEAD_DIM
    scale = HEAD_DIM ** -0.5
    q_heads = (jnp.where(lo, a, 0.0), jnp.where(lo, pltpu.roll(a, HEAD_DIM, 1), 0.0),
               jnp.where(lo, bv, 0.0))
    q3 = (jnp.concatenate(q_heads, axis=0) * scale).astype(BF16)
    row_q = q0 + lax.broadcasted_iota(jnp.int32, (TQ, 1), 0)
    row_q3 = jnp.concatenate([row_q] * R, axis=0)

    kc = kc_ref[0, 0]
    nc = kc.shape[0]
    s = _nt_dot(q3, kc)
    n_idx = lax.broadcasted_iota(jnp.int32, (R * TQ, nc), 1)
    vis = (n_idx * CMP_STRIDE + (CMP_LEN - 1) <= row_q3) & (n_idx < n_cmp)
    s = jnp.where(vis, s, MASK_OFF)
    m = jnp.max(s, axis=1, keepdims=True)
    p = jnp.where(vis, jnp.exp(s - m), 0.0)
    d = jnp.sum(p, axis=1, keepdims=True)
    p_c = p / jnp.where(d > 0, d, 1.0)
    o_c = jnp.dot(p_c.astype(BF16), kc, preferred_element_type=F32)

    p_sum = p_c[0:TQ]
    for r in range(1, R):
        p_sum = p_sum + p_c[r * TQ:(r + 1) * TQ]
    p_hi = p_sum.astype(BF16)
    p_lo = (p_sum - p_hi.astype(F32)).astype(BF16)
    covt = covt_ref[...]
    imp = _nt_dot(covt, p_hi) + _nt_dot(covt, p_lo)
    j = lax.broadcasted_iota(jnp.int32, (n_sel, TQ), 0)
    cur = jnp.right_shift(q0 + lax.broadcasted_iota(jnp.int32, (n_sel, TQ), 1),
                          SEL_BLK.bit_length() - 1)
    forced = (j == 0) | (j == cur) | (j == cur - 1)
    imp = jnp.where(forced, jnp.inf, jnp.where(j > cur, -jnp.inf, imp))
    rank = jnp.zeros((n_sel, TQ), F32)
    for b in range(n_sel):
        row = imp[b:b + 1, :]
        ahead = (row > imp) | ((row == imp) & (j > b))
        rank = rank + jnp.where(ahead, 1.0, 0.0)
    sel_off = jnp.where(rank < float(min(N_SEL, n_sel)), 0.0, MASK_OFF)
    sel_off = jnp.concatenate([sel_off, jnp.zeros((LANES - n_sel, TQ), F32)], axis=0)
    mask_q = sel_off.T.astype(BF16)
    qaug_ref[:, :LANES] = q3
    qaug_ref[:, LANES:] = jnp.concatenate([mask_q] * R, axis=0)

    def reset():
        m_ref[...] = jnp.full(m_ref.shape, MASK_OFF, F32)
        l_ref[...] = jnp.zeros(l_ref.shape, F32)
        acc_ref[...] = jnp.zeros(acc_ref.shape, F32)

    def finish():
        return acc_ref[...] / l_ref[...]

    reset()
    n_full = q0 // NSA_TK

    def sel_tile(t, masked):
        k0 = pl.multiple_of(t * NSA_TK, NSA_TK)
        kt = ks_ref[pl.ds(k0, NSA_TK), :]
        kcat = jnp.concatenate([kt, et_ref[pl.ds(k0, NSA_TK), :]], axis=1)
        if masked:
            causal = k0 + lax.broadcasted_iota(jnp.int32, (TQ, NSA_TK), 1) <= row_q
        for r in range(R):
            rows = pl.ds(r * TQ, TQ)
            s = _nt_dot(qaug_ref[rows, :], kcat)
            if masked:
                s = jnp.where(causal, s, MASK_OFF)
            _flash_step(s, kt, rows, m_ref, l_ref, acc_ref)

    def sel_loop(t, c):
        sel_tile(t, False)
        return c

    lax.fori_loop(0, n_full, sel_loop, 0)
    sel_tile(n_full, True)
    o_s = finish()

    band = WINDOW + TQ
    w0 = pl.multiple_of(jnp.maximum(q0 - WINDOW, 0), NSA_TW)
    kt = kw_ref[pl.ds(w0, band), :]
    dist = row_q - (w0 + lax.broadcasted_iota(jnp.int32, (TQ, band), 1))
    inside = (dist >= 0) & (dist <= WINDOW)
    o_w = []
    for r in range(R):
        s = jnp.where(inside, _nt_dot(qaug_ref[pl.ds(r * TQ, TQ), :LANES], kt), MASK_OFF)
        p = jnp.exp(s - jnp.max(s, axis=1, keepdims=True))
        o_w.append(jnp.dot(p.astype(BF16), kt, preferred_element_type=F32)
                   / jnp.sum(p, axis=1, keepdims=True))
    o_w = jnp.concatenate(o_w, axis=0)

    gates = jax.nn.sigmoid(bv)
    outs = []
    for r in range(R):
        c0 = HEAD_DIM + 3 * r
        rows = slice(r * TQ, (r + 1) * TQ)
        outs.append(gates[:, c0:c0 + 1] * o_c[rows] + gates[:, c0 + 1:c0 + 2] * o_s[rows]
                    + gates[:, c0 + 2:c0 + 3] * o_w[rows])
    o_ref[0, :, :LANES] = jnp.where(lo, pltpu.roll(outs[0], HEAD_DIM, 1), outs[1])
    o_ref[0, :, LANES:] = jnp.where(lo, pltpu.roll(outs[2], HEAD_DIM, 1), 0.0)


def nsa_prompt_pallas(p, ksw, kc, covt, B, T, n_cmp):
    n_sel = covt.shape[0]
    nq = T // NSA_TQ
    rows = HEADS_PER_KV * NSA_TQ
    e_t = (jnp.arange(T)[:, None] // SEL_BLK == jnp.arange(LANES)[None, :]).astype(BF16)
    body = functools.partial(_nsa_prompt_body, n_cmp=n_cmp, n_sel=n_sel)
    return pl.pallas_call(
        body,
        grid=(B, N_KV, nq),
        in_specs=[pl.BlockSpec((NSA_TQ, SLAB), lambda b, g, i: (b * nq + i, g)),
                  pl.BlockSpec((T, LANES), lambda b, g, i: (b, g)),
                  pl.BlockSpec((T, LANES), lambda b, g, i: (b, N_KV + g)),
                  pl.BlockSpec((T, LANES), lambda b, g, i: (0, 0)),
                  pl.BlockSpec((1, 1) + kc.shape[2:], lambda b, g, i: (b, g, 0, 0)),
                  pl.BlockSpec(covt.shape, lambda b, g, i: (0, 0))],
        out_specs=pl.BlockSpec((1, NSA_TQ, SLAB), lambda b, g, i: (g, b * nq + i, 0)),
        out_shape=jax.ShapeDtypeStruct((N_KV, B * T, SLAB), F32),
        scratch_shapes=[pltpu.VMEM((rows, SLAB), BF16), pltpu.VMEM((rows, 1), F32),
                        pltpu.VMEM((rows, 1), F32), pltpu.VMEM((rows, LANES), F32)],
        compiler_params=_params("parallel", "parallel", "arbitrary"),
        name="nsa_prompt",
    )(p, ksw, ksw, e_t, kc, covt)


def nsa_kv_weight(w_kv):
    D = w_kv.shape[0]
    w = w_kv.reshape(D, 6, N_KV, HEAD_DIM)
    pair = lambda c: w[:, c:c + 2].transpose(0, 2, 1, 3).reshape(D, N_KV * 2 * HEAD_DIM)
    return jnp.concatenate([pair(2), pair(4)], axis=1).astype(BF16)


def nsa_prompt_operands(kc_all, B, T):
    n_sel = T // SEL_BLK
    n_chunk = T // CMP_STRIDE
    n_cmp = n_chunk - CMP_R + 1
    kcv = kc_all.reshape(2, B, n_chunk, N_KV, HEAD_DIM).transpose(1, 3, 2, 0, 4)
    kcv = kcv.reshape(B, N_KV, n_chunk, 2 * HEAD_DIM).astype(BF16)
    c_start = jnp.arange(n_chunk) * CMP_STRIDE
    s_start = jnp.arange(n_sel) * SEL_BLK
    covt = ((c_start[None, :] < s_start[:, None] + SEL_BLK)
            & (c_start[None, :] + CMP_LEN - 1 >= s_start[:, None])
            & (jnp.arange(n_chunk)[None, :] < n_cmp)).astype(BF16)
    return kcv, covt


CMP_R = CMP_LEN // CMP_STRIDE
KV_W = N_KV * HEAD_DIM


def _compress_body(xa_ref, xb_ref, w1_ref, pe_ref, b1_ref, w2_ref, b2_ref, o_ref):
    m = o_ref.shape[1]
    acc = [None] * CMP_R
    for s in range(CMP_STRIDE):
        rows_s = pl.ds(s, m, stride=CMP_STRIDE)
        xs = jnp.concatenate([xa_ref[rows_s, :], xb_ref[rows_s, :]], axis=1)
        for r in range(CMP_R):
            t = jnp.dot((xs + pe_ref[0, r, s:s + 1, :]).astype(BF16), w1_ref[0, r, s],
                        preferred_element_type=F32)
            acc[r] = t if acc[r] is None else acc[r] + t
    hid = b1_ref[0] + acc[0]
    for r in range(1, CMP_R):
        hid = hid + pltpu.roll(acc[r], m - r, 0)
    o_ref[0] = jnp.dot(jax.nn.gelu(hid).astype(BF16), w2_ref[0],
                       preferred_element_type=F32) + b2_ref[0]


def compress_operands(cmp_pe, cmp_w1, cmp_b1, cmp_w2, cmp_b2):
    eye = jnp.eye(N_KV, dtype=F32)
    w1 = cmp_w1.reshape(2, CMP_R, CMP_STRIDE, HEAD_DIM, CMP_HID)
    w1 = jnp.einsum('krsdh,gG->krsgdGh', w1, eye).reshape(
        2, CMP_R, CMP_STRIDE, KV_W, N_KV * CMP_HID).astype(BF16)
    pe = jnp.tile(cmp_pe.reshape(2, CMP_R, CMP_STRIDE, HEAD_DIM), (1, 1, 1, N_KV))
    b1 = jnp.tile(cmp_b1, (1, N_KV)).reshape(2, 1, N_KV * CMP_HID)
    w2 = jnp.einsum('khd,gG->kghGd', cmp_w2, eye).reshape(2, N_KV * CMP_HID, KV_W).astype(BF16)
    b2 = jnp.tile(cmp_b2, (1, N_KV)).reshape(2, 1, KV_W)
    return w1, pe, b1, w2, b2


def compress(x2d, n_seq, seq_rows, seqs_per_step, ops):
    w1, pe, b1, w2, b2 = ops
    rows = seqs_per_step * seq_rows
    m = rows // CMP_STRIDE
    return pl.pallas_call(
        _compress_body,
        grid=(2, n_seq // seqs_per_step),
        in_specs=[pl.BlockSpec((rows, LANES), lambda k, i: (i, 2 * k)),
                  pl.BlockSpec((rows, LANES), lambda k, i: (i, 2 * k + 1)),
                  pl.BlockSpec((1,) + w1.shape[1:], lambda k, i: (k, 0, 0, 0, 0)),
                  pl.BlockSpec((1,) + pe.shape[1:], lambda k, i: (k, 0, 0, 0)),
                  pl.BlockSpec((1,) + b1.shape[1:], lambda k, i: (k, 0, 0)),
                  pl.BlockSpec((1,) + w2.shape[1:], lambda k, i: (k, 0, 0)),
                  pl.BlockSpec((1,) + b2.shape[1:], lambda k, i: (k, 0, 0))],
        out_specs=pl.BlockSpec((1, m, KV_W), lambda k, i: (k, i, 0)),
        out_shape=jax.ShapeDtypeStruct((2, n_seq * seq_rows // CMP_STRIDE, KV_W), F32),
        compiler_params=_params("arbitrary", "arbitrary"),
        name="compress",
    )(x2d, x2d, w1, pe, b1, w2, b2)


def _softmax_two(s_a, s_b):
    m = jnp.maximum(jnp.max(s_a, axis=1, keepdims=True), jnp.max(s_b, axis=1, keepdims=True))
    p_a, p_b = jnp.exp(s_a - m), jnp.exp(s_b - m)
    d = jnp.sum(p_a, axis=1, keepdims=True) + jnp.sum(p_b, axis=1, keepdims=True)
    return p_a, p_b, d


def _nsa_sample_body(p_ref, past_ref, new_ref, win_ref, kc_ref, cov_ref, esel_ref, o_ref,
                     *, past_len, n_cmp, n_sel):
    S = p_ref.shape[0]
    G, R = N_KV, HEADS_PER_KV
    w_buf = win_ref.shape[0]
    slab = p_ref[...]
    lane = lax.broadcasted_iota(jnp.int32, (S, LANES), 1)
    lo = lane < HEAD_DIM
    zero = jnp.zeros((S, LANES), F32)

    def half(x, src_hi, dst_hi):
        x = x if src_hi == dst_hi else pltpu.roll(x, HEAD_DIM, 1)
        return jnp.where(lo != dst_hi, x, 0.0)

    q_rows, gate_cols = [], ([], [], [])
    for g in range(G):
        a = slab[:, g * SLAB:g * SLAB + LANES]
        bv = slab[:, g * SLAB + LANES:(g + 1) * SLAB]
        sig = jax.nn.sigmoid(bv)
        for r in range(R):
            piece = half(a if r < 2 else bv, r == 1, g % 2 == 1)
            tiles = [zero, zero]
            tiles[g // 2] = piece
            q_rows.append(jnp.concatenate(tiles, axis=1))
            for br in range(3):
                c = HEAD_DIM + 3 * r + br
                gate_cols[br].append(sig[:, c:c + 1])
    q_all = (jnp.concatenate(q_rows, axis=0) * HEAD_DIM ** -0.5).astype(BF16)
    gate = [jnp.concatenate(cols, axis=0) for cols in gate_cols]
    n_rows = G * R * S
    t_col = lax.broadcasted_iota(jnp.int32, (S, 1), 0)
    t_row = jnp.concatenate([t_col] * (G * R), axis=0)
    q_pos = past_len + t_row

    kck = kc_ref[0].astype(BF16)
    kcv = kc_ref[1].astype(BF16)
    nc = kck.shape[0]
    s = _nt_dot(q_all, kck)
    n_idx = lax.broadcasted_iota(jnp.int32, (n_rows, nc), 1)
    vis = (n_idx * CMP_STRIDE + (CMP_LEN - 1) <= q_pos) & (n_idx < n_cmp)
    s = jnp.where(vis, s, MASK_OFF)
    m = jnp.max(s, axis=1, keepdims=True)
    p = jnp.where(vis, jnp.exp(s - m), 0.0)
    d = jnp.sum(p, axis=1, keepdims=True)
    p_c = p / jnp.where(d > 0, d, 1.0)
    o_c = jnp.dot(p_c.astype(BF16), kcv, preferred_element_type=F32)

    p_sum = []
    for g in range(G):
        acc = p_c[g * R * S:(g * R + 1) * S]
        for r in range(1, R):
            acc = acc + p_c[(g * R + r) * S:(g * R + r + 1) * S]
        p_sum.append(acc)
    p_sum = jnp.concatenate(p_sum, axis=0)
    p_hi = p_sum.astype(BF16)
    p_lo = (p_sum - p_hi.astype(F32)).astype(BF16)
    cov = cov_ref[...]
    imp = (jnp.dot(p_hi, cov, preferred_element_type=F32)
           + jnp.dot(p_lo, cov, preferred_element_type=F32))
    j = lax.broadcasted_iota(jnp.int32, (G * S, LANES), 1)
    cur = jnp.right_shift(past_len + jnp.concatenate([t_col] * G, axis=0), SEL_BLK.bit_length() - 1)
    forced = (j == 0) | (j == cur) | (j == cur - 1)
    imp = jnp.where(forced, jnp.inf, jnp.where((j > cur) | (j >= n_sel), -jnp.inf, imp))
    rank = jnp.zeros((G * S, LANES), F32)
    for b in range(n_sel):
        col = imp[:, b:b + 1]
        rank = rank + jnp.where((col > imp) | ((col == imp) & (j > b)), 1.0, 0.0)
    sel_off = jnp.where(rank < float(min(N_SEL, n_sel)), 0.0, MASK_OFF)
    mask_rows = jnp.concatenate(
        [sel_off[g * S:(g + 1) * S] for g in range(G) for _ in range(R)], axis=0).astype(BF16)

    new = new_ref[...]
    pad = jnp.zeros((LANES - S, KV_W), F32)
    u = lax.broadcasted_iota(jnp.int32, (n_rows, LANES), 1)
    new_vis = u <= t_row

    def new_keys(slot):
        return jnp.concatenate([new[:, slot * KV_W:(slot + 1) * KV_W], pad], axis=0).astype(BF16)

    esel = esel_ref[...]
    s_p = (_nt_dot(q_all, past_ref[:, :KV_W].astype(BF16))
           + jnp.dot(mask_rows, esel[:, :past_len], preferred_element_type=F32))
    s_n = (_nt_dot(q_all, new_keys(2))
           + jnp.dot(mask_rows, esel[:, past_len:], preferred_element_type=F32))
    s_n = jnp.where(new_vis, s_n, MASK_OFF)
    p_p, p_n, d = _softmax_two(s_p, s_n)
    o_s = (jnp.dot(p_p.astype(BF16), past_ref[:, KV_W:].astype(BF16), preferred_element_type=F32)
           + jnp.dot(p_n.astype(BF16), new_keys(3), preferred_element_type=F32)) / d

    s_w = _nt_dot(q_all, win_ref[:, :KV_W].astype(BF16))
    w_pos = (past_len - w_buf) + lax.broadcasted_iota(jnp.int32, (n_rows, w_buf), 1)
    dist = q_pos - w_pos
    s_w = jnp.where((w_pos >= 0) & (dist >= 0) & (dist <= WINDOW), s_w, MASK_OFF)
    s_wn = jnp.where(new_vis, _nt_dot(q_all, new_keys(4)), MASK_OFF)
    p_w, p_wn, d = _softmax_two(s_w, s_wn)
    o_w = (jnp.dot(p_w.astype(BF16), win_ref[:, KV_W:].astype(BF16), preferred_element_type=F32)
           + jnp.dot(p_wn.astype(BF16), new_keys(5), preferred_element_type=F32)) / d

    out = gate[0] * o_c + gate[1] * o_s + gate[2] * o_w

    for g in range(G):
        def head(r, dst_hi):
            rows = out[(g * R + r) * S:(g * R + r + 1) * S]
            return half(rows[:, (g // 2) * LANES:(g // 2 + 1) * LANES], g % 2 == 1, dst_hi)
        o_ref[:, g * SLAB:g * SLAB + LANES] = head(0, False) + head(1, True)
        o_ref[:, g * SLAB + LANES:(g + 1) * SLAB] = head(2, False)


def nsa_sample_pallas(p, row0, past2d, kv, cache_win2d, kc_all, Bs, S, past_len):
    w_buf = cache_win2d.shape[0] // Bs
    T = past_len + S
    n_sel = -(-T // SEL_BLK)
    nc = kc_all.shape[1] // Bs
    n_cmp = T // CMP_STRIDE - CMP_R + 1
    c_start = jnp.arange(nc) * CMP_STRIDE
    s_start = jnp.arange(LANES) * SEL_BLK
    cov = ((c_start[:, None] < s_start[None, :] + SEL_BLK)
           & (c_start[:, None] + CMP_LEN - 1 >= s_start[None, :])
           & (jnp.arange(nc)[:, None] < n_cmp) & (jnp.arange(LANES)[None, :] < n_sel)).astype(BF16)
    k_pos = jnp.arange(past_len + LANES)
    esel = ((k_pos[None, :] // SEL_BLK == jnp.arange(LANES)[:, None])
            & (k_pos[None, :] < T)).astype(BF16)
    body = functools.partial(_nsa_sample_body, past_len=past_len, n_cmp=n_cmp, n_sel=n_sel)
    blk0 = row0 // S
    return pl.pallas_call(
        body,
        grid=(Bs,),
        in_specs=[pl.BlockSpec((S, N_KV * SLAB), lambda b: (blk0 + b, 0)),
                  pl.BlockSpec((past_len, 2 * KV_W), lambda b: (b, 1)),
                  pl.BlockSpec((S, 6 * KV_W), lambda b: (blk0 + b, 0)),
                  pl.BlockSpec((w_buf, 2 * KV_W), lambda b: (b, 0)),
                  pl.BlockSpec((2, nc, KV_W), lambda b: (0, b, 0)),
                  pl.BlockSpec(cov.shape, lambda b: (0, 0)),
                  pl.BlockSpec(esel.shape, lambda b: (0, 0))],
        out_specs=pl.BlockSpec((S, N_KV * SLAB), lambda b: (b, 0)),
        out_shape=jax.ShapeDtypeStruct((Bs * S, N_KV * SLAB), F32),
        compiler_params=_params("parallel"),
        name="nsa_sample",
    )(p, past2d, kv, cache_win2d, kc_all, cov, esel)


RG_TB = 256


def _rglru_body(*refs, seg, carry):
    if carry:
        (gate_ref, u_ref, cw_ref, cb_ref, wai_ref, bai_ref, sp_ref,
         y_ref, hl_ref, prev_ref, h_ref) = refs
    else:
        (gate_ref, u_ref, a1_ref, a2_ref, a3_ref, h0_ref, cw_ref, cb_ref, wai_ref, bai_ref,
         sp_ref, y_ref, hall_ref) = refs
        alt_refs = (a1_ref, a2_ref, a3_ref)
    TB = u_ref.shape[0]
    u = u_ref[...]
    t = lax.broadcasted_iota(jnp.int32, (TB, 1), 0) & (seg - 1)
    if carry:
        @pl.when(pl.program_id(1) == 0)
        def _():
            prev_ref[...] = jnp.zeros(prev_ref.shape, F32)
            h_ref[...] = jnp.zeros(h_ref.shape, F32)
        prev = prev_ref[...]

    xc = u * cw_ref[CONV_W - 1:CONV_W, :] + cb_ref[...]
    for k in range(1, CONV_W):
        back = pltpu.roll(prev, k, 0) if carry else alt_refs[k - 1][...]
        xc = xc + jnp.where(t >= k, pltpu.roll(u, k, 0), back) * cw_ref[CONV_W - 1 - k:CONV_W - k, :]

    ri = jnp.dot(xc.astype(BF16), wai_ref[...], preferred_element_type=F32) + bai_ref[...]
    r = jax.nn.sigmoid(ri[:, :D_RNN])
    i_gate = jax.nn.sigmoid(ri[:, D_RNN:])
    log_a = -RG_C * r * sp_ref[...]
    a = jnp.exp(log_a)
    x2 = 2.0 * log_a
    e2 = jnp.exp(x2)
    one = e2 == 1.0
    expm1 = jnp.where(one, x2, (e2 - 1.0) * x2 / jnp.where(one, 1.0, jnp.log(e2)))
    b = jnp.sqrt(-expm1) * (i_gate * xc)

    d = 1
    while d < seg:
        ok = t >= d
        b = jnp.where(ok, a * pltpu.roll(b, d, 0) + b, b)
        a = jnp.where(ok, a * pltpu.roll(a, d, 0), a)
        d *= 2
    h = a * (h_ref[0:1, :] if carry else h0_ref[...]) + b
    y_ref[...] = h * jax.nn.gelu(gate_ref[...])
    if carry:
        prev_ref[...] = u
        h_last = h[TB - 1:TB, :]
        h_ref[...] = jnp.broadcast_to(h_last, h_ref.shape)
        hl_ref[0] = h_last
    else:
        hall_ref[...] = h


def rglru_operands(conv_w, conv_b, w_a, b_a, w_i, b_i, lam):
    eye = jnp.eye(N_RG_BLOCKS, dtype=F32)
    bd = lambda w: jnp.einsum('ncd,nm->ncmd', w, eye).reshape(D_RNN, D_RNN)
    wai = jnp.concatenate([bd(w_a), bd(w_i)], axis=1).astype(BF16)
    bai = jnp.concatenate([b_a, b_i]).reshape(1, 2 * D_RNN)
    sp = jax.nn.softplus(-lam.astype(F32)).reshape(1, D_RNN)
    return conv_w, conv_b.reshape(1, D_RNN), wai, bai, sp


def _rg_param_specs(n_grid):
    z = (lambda *_: (0, 0))
    return [pl.BlockSpec((CONV_W, D_RNN), z), pl.BlockSpec((1, D_RNN), z),
            pl.BlockSpec((D_RNN, 2 * D_RNN), z), pl.BlockSpec((1, 2 * D_RNN), z),
            pl.BlockSpec((1, D_RNN), z)]


def rglru_prompt(p, B, T, ops):
    nblk = T // RG_TB
    body = functools.partial(_rglru_body, seg=RG_TB, carry=True)
    return pl.pallas_call(
        body,
        grid=(B, nblk),
        in_specs=[pl.BlockSpec((RG_TB, D_RNN), lambda b, i: (b * nblk + i, 0)),
                  pl.BlockSpec((RG_TB, D_RNN), lambda b, i: (b * nblk + i, 1))] + _rg_param_specs(2),
        out_specs=[pl.BlockSpec((RG_TB, D_RNN), lambda b, i: (b * nblk + i, 0)),
                   pl.BlockSpec((1, 1, D_RNN), lambda b, i: (b, 0, 0))],
        out_shape=[jax.ShapeDtypeStruct((B * T, D_RNN), F32),
                   jax.ShapeDtypeStruct((B, 1, D_RNN), F32)],
        scratch_shapes=[pltpu.VMEM((RG_TB, D_RNN), F32), pltpu.VMEM((8, D_RNN), F32)],
        compiler_params=_params("parallel", "arbitrary"),
        name="rglru_prompt",
    )(p, p, *ops)


def rglru_sample(p, row0, B, S, conv_buf, h0, ops, tb=512):
    ext = jnp.concatenate([conv_buf, jnp.zeros((B, S, D_RNN), F32)], axis=1)
    alts = [ext[:, CONV_W - 1 - k:CONV_W - 1 - k + S].reshape(B * S, D_RNN)
            for k in range(1, CONV_W)]
    h0x = jnp.repeat(h0, S, axis=0)
    blk0 = row0 // tb
    body = functools.partial(_rglru_body, seg=S, carry=False)
    row = lambda i: (i, 0)
    return pl.pallas_call(
        body,
        grid=(B * S // tb,),
        in_specs=[pl.BlockSpec((tb, D_RNN), lambda i: (blk0 + i, 0)),
                  pl.BlockSpec((tb, D_RNN), lambda i: (blk0 + i, 1))]
                 + [pl.BlockSpec((tb, D_RNN), row)] * 4 + _rg_param_specs(1),
        out_specs=[pl.BlockSpec((tb, D_RNN), row), pl.BlockSpec((tb, D_RNN), row)],
        out_shape=[jax.ShapeDtypeStruct((B * S, D_RNN), F32)] * 2,
        compiler_params=_params("parallel"),
        name="rglru_sample",
    )(p, p, *alts, h0x, *ops)


def _mem_attn_body(q_ref, kv_ref, o_ref):
    TB = q_ref.shape[0]
    q = q_ref[...] * MEM_HD ** -0.5
    head = jnp.right_shift(lax.broadcasted_iota(jnp.int32, (TB, MEM_W), 1), MEM_HD.bit_length() - 1)
    q_all = jnp.concatenate([jnp.where(head == h, q, 0.0) for h in range(MEM_HEADS)],
                            axis=0).astype(BF16)
    k = kv_ref[:, :MEM_W].astype(BF16)
    v = kv_ref[:, MEM_W:].astype(BF16)
    s = _nt_dot(q_all, k)
    p = jnp.exp(s - jnp.max(s, axis=1, keepdims=True))
    o = jnp.dot(p.astype(BF16), v, preferred_element_type=F32) / jnp.sum(p, axis=1, keepdims=True)
    out = jnp.where(head == 0, o[0:TB], 0.0)
    for h in range(1, MEM_HEADS):
        out = out + jnp.where(head == h, o[h * TB:(h + 1) * TB], 0.0)
    o_ref[...] = out


def mem_attend_pallas(p, row0, q_col_blk, n_seq, seq_rows, tb, kv2d, kv_blk0, kv_col_blk):
    nblk = seq_rows // tb
    blk0 = row0 // tb
    return pl.pallas_call(
        _mem_attn_body,
        grid=(n_seq, nblk),
        in_specs=[pl.BlockSpec((tb, MEM_W), lambda b, i: (blk0 + b * nblk + i, q_col_blk)),
                  pl.BlockSpec((N_MEM, 2 * MEM_W), lambda b, i: (kv_blk0 + b, kv_col_blk))],
        out_specs=pl.BlockSpec((tb, MEM_W), lambda b, i: (b * nblk + i, 0)),
        out_shape=jax.ShapeDtypeStruct((n_seq * seq_rows, MEM_W), F32),
        compiler_params=_params("parallel", "parallel"),
        name="mem_attend",
    )(p, kv2d)


def masked_softmax(s, mask):
    s = jnp.where(mask, s, -jnp.inf)
    m = jnp.max(s, axis=-1, keepdims=True)
    m = jnp.where(jnp.isfinite(m), m, 0.0)
    p = jnp.exp(s - m)
    d = jnp.sum(p, axis=-1, keepdims=True)
    return p / jnp.where(d > 0, d, 1.0)


def mem_attend(q, mem_kv):
    B, T, _ = q.shape
    qh = q.reshape(B, T, MEM_HEADS, MEM_HD)
    k, v = mem_kv[:, :, 0], mem_kv[:, :, 1]
    s = jnp.einsum('bthd,bmhd->bhtm', qh, k).astype(F32) * (MEM_HD ** -0.5)
    p = jax.nn.softmax(s, axis=-1)
    o = jnp.einsum('bhtm,bmhd->bthd', p.astype(v.dtype), v)
    return o.reshape(B, T, MEM_W)


def _lin_combine(c1, c2):
    a1, b1 = c1
    a2, b2 = c2
    return a1 * a2, a2 * b1 + b2


def rglru_branch(u, conv_buf, h0, conv_w, conv_b, w_a, b_a, w_i, b_i, lam):
    B, T, C = u.shape
    full = jnp.concatenate([conv_buf.astype(u.dtype), u], axis=1)
    xc = lax.conv_general_dilated(full, conv_w.astype(u.dtype)[:, None, :], (1,), 'VALID',
                                  dimension_numbers=('NWC', 'WIO', 'NWC'),
                                  feature_group_count=C) + conv_b
    xb = xc.reshape(B, T, N_RG_BLOCKS, RG_BW)
    r = jax.nn.sigmoid(jnp.einsum('btnc,ncd->btnd', xb, w_a).reshape(B, T, C) + b_a)
    i = jax.nn.sigmoid(jnp.einsum('btnc,ncd->btnd', xb, w_i).reshape(B, T, C) + b_i)
    log_a = -RG_C * r.astype(F32) * jax.nn.softplus(-lam.astype(F32))
    a = jnp.exp(log_a)
    b = jnp.sqrt(-jnp.expm1(2.0 * log_a)) * (i * xc).astype(F32)
    b = b.at[:, 0].add(a[:, 0] * h0.astype(F32))
    _, h = lax.associative_scan(_lin_combine, (a, b), axis=1)
    return h.astype(u.dtype), h[:, -1].astype(u.dtype), full[:, T:]


def compress_tokens(kseq, pe, w1, b1, w2, b2):
    B, T, G, Dh = kseq.shape
    R = CMP_LEN // CMP_STRIDE
    n_chunk = T // CMP_STRIDE
    n_cmp = n_chunk - R + 1
    ch = kseq[:, :n_chunk * CMP_STRIDE].reshape(B, n_chunk, CMP_STRIDE, G, Dh)
    pe_r = pe.reshape(R, CMP_STRIDE, Dh)
    w1_r = w1.reshape(R, CMP_STRIDE, Dh, CMP_HID)
    hid = b1
    for r in range(R):
        hid = hid + jnp.einsum('bnsgd,sdh->bngh', ch[:, r:r + n_cmp] + pe_r[r][:, None, :], w1_r[r])
    out = jnp.einsum('bngh,hd->bngd', jax.nn.gelu(hid), w2) + b2
    ends = jnp.arange(n_cmp) * CMP_STRIDE + (CMP_LEN - 1)
    return out, ends


def nsa_context(kv4, cmp_pe, cmp_w1, cmp_b1, cmp_w2, cmp_b2):
    kc, ends = compress_tokens(kv4[:, :, 0], cmp_pe[0], cmp_w1[0], cmp_b1[0], cmp_w2[0], cmp_b2[0])
    vc, _ = compress_tokens(kv4[:, :, 1], cmp_pe[1], cmp_w1[1], cmp_b1[1], cmp_w2[1], cmp_b2[1])
    B, T = kv4.shape[:2]
    n_sel = -(-T // SEL_BLK)
    sel = jnp.pad(kv4[:, :, 2:4], ((0, 0), (0, n_sel * SEL_BLK - T), (0, 0), (0, 0), (0, 0)))
    sel = sel.reshape(B, n_sel, SEL_BLK, 2, N_KV, HEAD_DIM).transpose(3, 0, 4, 1, 2, 5)
    return kc, vc, ends, sel[0], sel[1]


def nsa_attend(q, gates, q_pos, kc, vc, c_end, ksb, vsb, kw, vw, w_pos):
    B, Qb, H, Dh = q.shape
    G = kc.shape[2]
    R = H // G
    n_sel = ksb.shape[2]
    scale = Dh ** -0.5
    qg = q.reshape(B, Qb, G, R, Dh)
    s_c = jnp.einsum('bqgrd,bngd->bgrqn', qg, kc).astype(F32) * scale
    p_c = masked_softmax(s_c, c_end[None, :] <= q_pos[:, None])
    o_c = jnp.einsum('bgrqn,bngd->bqgrd', p_c.astype(vc.dtype), vc)
    c_start = c_end - (CMP_LEN - 1)
    s_start = jnp.arange(n_sel) * SEL_BLK
    cover = ((c_start[:, None] < s_start[None, :] + SEL_BLK) & (c_end[:, None] >= s_start[None, :])).astype(F32)
    imp = jnp.einsum('bgrqn,nj->bgqj', p_c, cover)
    cur = q_pos // SEL_BLK
    j = jnp.arange(n_sel)
    forced = (j[None, :] == 0) | (j[None, :] == cur[:, None]) | (j[None, :] == cur[:, None] - 1)
    future = j[None, :] > cur[:, None]
    imp = jnp.where(forced, jnp.inf, jnp.where(future, -jnp.inf, imp))
    _, idx = lax.top_k(imp, min(N_SEL, n_sel))
    gather = jax.vmap(jax.vmap(lambda blocks, ids: blocks[ids]))
    ks = gather(ksb, idx)
    vs = gather(vsb, idx)
    n_top = idx.shape[-1]
    k_pos = idx[..., None] * SEL_BLK + jnp.arange(SEL_BLK)
    m_s = (k_pos <= q_pos[:, None, None]).reshape(B, G, 1, Qb, n_top * SEL_BLK)
    s_s = jnp.einsum('bqgrd,bgqksd->bgrqks', qg, ks).astype(F32) * scale
    p_s = masked_softmax(s_s.reshape(B, G, R, Qb, n_top * SEL_BLK), m_s)
    o_s = jnp.einsum('bgrqm,bgqmd->bqgrd', p_s.astype(vs.dtype), vs.reshape(B, G, Qb, n_top * SEL_BLK, Dh))
    dist = q_pos[:, None] - w_pos[None, :]
    m_w = (w_pos[None, :] >= 0) & (dist >= 0) & (dist <= WINDOW)
    s_w = jnp.einsum('bqgrd,bngd->bgrqn', qg, kw).astype(F32) * scale
    p_w = masked_softmax(s_w, m_w)
    o_w = jnp.einsum('bgrqn,bngd->bqgrd', p_w.astype(vw.dtype), vw)
    g = gates.reshape(B, Qb, G, R, 3)
    out = g[..., 0:1] * o_c + g[..., 1:2] * o_s + g[..., 2:3] * o_w
    return out.reshape(B, Qb, H * Dh)


def nsa_prompt(q, gates, ctx, k_win, v_win):
    B, T, H, Dh = q.shape
    n_blk = T // Q_BLK
    zpad = jnp.zeros((B, WINDOW) + k_win.shape[2:], k_win.dtype)
    kw_pad = jnp.concatenate([zpad, k_win], axis=1)
    vw_pad = jnp.concatenate([zpad.astype(v_win.dtype), v_win], axis=1)
    qb = q.reshape(B, n_blk, Q_BLK, H, Dh).swapaxes(0, 1)
    gb = gates.reshape(B, n_blk, Q_BLK, H, 3).swapaxes(0, 1)
    band = jnp.arange(WINDOW + Q_BLK)

    def block(args):
        qi, gi, i = args
        start = i * Q_BLK
        kw = lax.dynamic_slice_in_dim(kw_pad, start, WINDOW + Q_BLK, axis=1)
        vw = lax.dynamic_slice_in_dim(vw_pad, start, WINDOW + Q_BLK, axis=1)
        return nsa_attend(qi, gi, start + jnp.arange(Q_BLK), *ctx, kw, vw, start - WINDOW + band)

    out = lax.map(block, (qb, gb, jnp.arange(n_blk)))
    return out.swapaxes(0, 1).reshape(B, T, H * Dh)


def nsa_sample(q, gates, ctx, k_win, v_win, pos0, w_first):
    B, S, H, Dh = q.shape
    w_pos = w_first + jnp.arange(k_win.shape[1])
    q_pos = pos0 + jnp.arange(S)

    def step(args):
        qi, gi, t = args
        return nsa_attend(qi[:, None], gi[:, None], t[None], *ctx, k_win, v_win, w_pos)

    out = lax.map(step, (q.swapaxes(0, 1), gates.swapaxes(0, 1), q_pos))
    return out[:, :, 0].swapaxes(0, 1)


def kernel(x_prompt, x_sample, cache_kv, cache_win, cache_mem_kv, state_rg_h, state_rg_conv,
           page_table, mem_prompt, ln_g, ln_b, w_in_a, conv_w, conv_b, w_rg_a, b_rg_a,
           w_rg_i, b_rg_i, rg_lambda, w_out_a, w_kv_shared, cmp_pe, cmp_w1, cmp_b1, cmp_w2,
           cmp_b2, w_in_b, w_out_b, w_mem_kv, w_router, b_router, w_gate_up, b_gate_up,
           w_down, b_down):
    hq = N_HEADS * HEAD_DIM
    Bp, Tp, D = x_prompt.shape
    Bs, Ss, _ = x_sample.shape
    Np, Ns = Bp * Tp, Bs * Ss
    past_len = page_table.shape[1] * PAGE_SIZE
    w_buf = cache_win.shape[1]
    cmp_params = (cmp_pe, cmp_w1, cmp_b1, cmp_w2, cmp_b2)

    def split(t):
        return t[:Np].reshape(Bp, Tp, -1), t[Np:].reshape(Bs, Ss, -1)

    def join(tp, ts):
        return jnp.concatenate([tp.reshape(Np, -1), ts.reshape(Ns, -1)], axis=0)

    w_mem_all = w_mem_kv.transpose(1, 0, 2).reshape(D, DEPTH * 2 * MEM_W).astype(BF16)
    mem_all = dense(mem_prompt.reshape(Bp * N_MEM, D), w_mem_all, tm=Bp * N_MEM // 2)
    mem_kv_prompt = mem_all.reshape(Bp, N_MEM, DEPTH, 2, MEM_HEADS, MEM_HD).transpose(2, 0, 1, 3, 4, 5)
    mem_kv_sample = cache_mem_kv.reshape(DEPTH * Bs * N_MEM, 2 * MEM_W)

    def mem_attention(p, q_col_blk, l):
        o_p = mem_attend_pallas(p, 0, q_col_blk, Bp, Tp, ROW_TILE, mem_all, 0, l)
        o_s = mem_attend_pallas(p, Np, q_col_blk, Bs, Ss, Ss, mem_kv_sample, l * Bs, 0)
        return jnp.concatenate([o_p, o_s], axis=0)

    x = join(x_prompt, x_sample)
    hs_p, hs_s, bufs_p, bufs_s = [], [], [], []
    for l in range(DEPTH):
        if l < N_A:
            p = dense(x, w_in_a[l].astype(BF16))
            rg_ops = rglru_operands(conv_w[l], conv_b[l], w_rg_a[l], b_rg_a[l], w_rg_i[l],
                                    b_rg_i[l], rg_lambda[l])
            y_p, h_last_p = rglru_prompt(p, Bp, Tp, rg_ops)
            y_s, h_all_s = rglru_sample(p, Np, Bs, Ss, state_rg_conv[l], state_rg_h[l], rg_ops)
            hs_p.append(h_last_p.reshape(Bp, D_RNN))
            hs_s.append(h_all_s.reshape(Bs, Ss, D_RNN)[:, -1])
            assert min(Tp, Ss) >= CONV_W - 1
            u_p, u_s = split(p[:, D_RNN:2 * D_RNN])
            bufs_p.append(u_p[:, Tp - (CONV_W - 1):])
            bufs_s.append(u_s[:, Ss - (CONV_W - 1):])
            o_mem = mem_attention(p, 2 * D_RNN // MEM_W, l)
            x = dense_post_norm(jnp.concatenate([y_p, y_s], axis=0), o_mem, w_out_a[l], x,
                                ln_g[l, 0], ln_b[l, 0])
        else:
            jb = l - N_A
            if l == N_A:
                kv = dense(x, w_kv_shared.astype(BF16))
                kv_p, kv_s = split(kv)
                kv_p = kv_p.reshape(Bp, Tp, 6, N_KV, HEAD_DIM)
                kv_s = kv_s.reshape(Bs, Ss, 6, N_KV, HEAD_DIM)
                cmp_ops = compress_operands(*cmp_params)
                kc_p = compress(kv, Bp, Tp, 1, cmp_ops)
                assert (past_len + Ss) // CMP_STRIDE * CMP_STRIDE == past_len
                past2d = cache_kv.reshape(-1, PAGE_SIZE, 4 * KV_W)[page_table].reshape(
                    Bs * past_len, 4 * KV_W)
                kc_s = compress(past2d, Bs, past_len, 2, cmp_ops)
                win_s = jnp.concatenate([cache_win, kv_s[:, :, 4:]], axis=1)
                kcv, covt = nsa_prompt_operands(kc_p, Bp, Tp)
                ksw = dense(x, nsa_kv_weight(w_kv_shared), rows=Np, out_dtype=BF16)
            hpg = hq // N_KV
            wq = w_in_b[jb][:, :hq].reshape(D, N_KV, hpg)
            wg = w_in_b[jb][:, hq:hq + 3 * N_HEADS].reshape(D, N_KV, 3 * HEADS_PER_KV)
            slabs = jnp.concatenate(
                [wq, wg, jnp.zeros((D, N_KV, SLAB - hpg - 3 * HEADS_PER_KV), F32)], axis=-1)
            w_in = jnp.concatenate([slabs.reshape(D, N_KV * SLAB), w_in_b[jb][:, hq + 3 * N_HEADS:]],
                                   axis=1).astype(BF16)
            p = dense(x, w_in)
            o_nsa_p = nsa_prompt_pallas(p, ksw, kcv, covt, Bp, Tp,
                                        Tp // CMP_STRIDE - CMP_R + 1)
            o_nsa_s = nsa_sample_pallas(p, Np, past2d, kv, cache_win.reshape(Bs * w_buf, 2 * KV_W),
                                        kc_s, Bs, Ss, past_len)
            o_nsa = jnp.concatenate(
                [o_nsa_p, o_nsa_s.reshape(Ns, N_KV, SLAB).transpose(1, 0, 2)], axis=1)
            o_mem = mem_attention(p, N_KV * SLAB // MEM_W, l)
            x = dense_post_norm_b(o_nsa, o_mem, w_out_b[jb], x, ln_g[l, 0], ln_b[l, 0])
        x = moe_post_norm(x, l, w_router[l], b_router[l], w_gate_up, b_gate_up, w_down, b_down,
                          ln_g[l, 1], ln_b[l, 1])

    y_prompt, y_sample = split(x)
    kv_rows_prompt = kv_p[:, :, :4]
    kv_rows_sample = kv_s[:, :, :4]
    win_prompt = kv_p[:, Tp - min(WINDOW, Tp):, 4:]
    win_sample = win_s[:, win_s.shape[1] - min(WINDOW, past_len + Ss):]
    return (y_prompt, y_sample, kv_rows_prompt, kv_rows_sample, win_prompt, win_sample,
            mem_kv_prompt, jnp.stack(hs_p), jnp.stack(hs_s), jnp.stack(bufs_p), jnp.stack(bufs_s))
```

```python
import functools

import jax
import jax.numpy as jnp
from jax import lax
from jax.experimental import pallas as pl
from jax.experimental.pallas import tpu as pltpu

D_MODEL = 1024
DEPTH = 4
PAGE_SIZE = 128
N_A = DEPTH // 2
D_RNN = 768
N_RG_BLOCKS = 8
CONV_W = 4
RG_C = 8.0
N_HEADS = 12
N_KV = 4
HEAD_DIM = 64
CMP_LEN = 32
CMP_STRIDE = 16
CMP_HID = 128
SEL_BLK = 64
N_SEL = 16
WINDOW = 512
N_MEM = 256
MEM_HEADS = 4
MEM_HD = 64
MEM_W = MEM_HEADS * MEM_HD
N_EXPERTS = 32
TOP_K = 4
D_EXPERT = 1024
SWIGLU_LIMIT = 7.0
SWIGLU_ALPHA = 1.702
DN_ALPHA = (2 * DEPTH) ** 0.25
LN_EPS = 1e-5

F32 = jnp.float32
BF16 = jnp.bfloat16

ROW_TILE = 512
MOE_BLK = 256
LANES = 128
VMEM_LIMIT = 56 * 1024 * 1024


def _params(*sem):
    return pltpu.CompilerParams(dimension_semantics=sem, vmem_limit_bytes=VMEM_LIMIT)


def _dense_body(x_ref, w_ref, o_ref):
    o_ref[...] = jnp.dot(x_ref[...].astype(BF16), w_ref[...],
                         preferred_element_type=F32).astype(o_ref.dtype)


def dense(x, w_bf16, tm=ROW_TILE, rows=None, out_dtype=F32):
    M, K = x.shape
    M = M if rows is None else rows
    N = w_bf16.shape[1]
    return pl.pallas_call(
        _dense_body,
        grid=(M // tm,),
        in_specs=[pl.BlockSpec((tm, K), lambda i: (i, 0)),
                  pl.BlockSpec((K, N), lambda i: (0, 0))],
        out_specs=pl.BlockSpec((tm, N), lambda i: (i, 0)),
        out_shape=jax.ShapeDtypeStruct((M, N), out_dtype),
        compiler_params=_params("parallel"),
        name="dense",
    )(x, w_bf16)


def _layer_norm_rows(z, g, b):
    mu = jnp.mean(z, axis=-1, keepdims=True)
    zc = z - mu
    var = jnp.mean(zc * zc, axis=-1, keepdims=True)
    return zc * lax.rsqrt(var + LN_EPS) * g + b


def _dense_ln_body(m1_ref, m2_ref, w1_ref, w2_ref, x_ref, g_ref, b_ref, o_ref):
    y = (jnp.dot(m1_ref[...].astype(BF16), w1_ref[...], preferred_element_type=F32)
         + jnp.dot(m2_ref[...].astype(BF16), w2_ref[...], preferred_element_type=F32))
    o_ref[...] = _layer_norm_rows(DN_ALPHA * x_ref[...] + y, g_ref[...], b_ref[...])


def dense_post_norm(m1, m2, w, x, g, b, tm=ROW_TILE):
    M, K1 = m1.shape
    K2 = m2.shape[1]
    D = w.shape[1]
    row = lambda i: (i, 0)
    fixed = lambda i: (0, 0)
    return pl.pallas_call(
        _dense_ln_body,
        grid=(M // tm,),
        in_specs=[pl.BlockSpec((tm, K1), row), pl.BlockSpec((tm, K2), row),
                  pl.BlockSpec((K1, D), fixed), pl.BlockSpec((K2, D), fixed),
                  pl.BlockSpec((tm, D), row), pl.BlockSpec((1, D), fixed),
                  pl.BlockSpec((1, D), fixed)],
        out_specs=pl.BlockSpec((tm, D), row),
        out_shape=jax.ShapeDtypeStruct((M, D), F32),
        compiler_params=_params("parallel"),
        name="dense_post_norm",
    )(m1, m2, w[:K1].astype(BF16), w[K1:].astype(BF16), x, g.reshape(1, D), b.reshape(1, D))


def _dense_ln_b_body(a_ref, m_ref, wa_ref, wm_ref, x_ref, g_ref, b_ref, o_ref):
    y = jnp.dot(m_ref[...].astype(BF16), wm_ref[...], preferred_element_type=F32)
    for grp in range(N_KV):
        y = y + jnp.dot(a_ref[grp].astype(BF16), wa_ref[grp], preferred_element_type=F32)
    o_ref[...] = _layer_norm_rows(DN_ALPHA * x_ref[...] + y, g_ref[...], b_ref[...])


def dense_post_norm_b(o_nsa, o_mem, w_out, x, g, b, tm=ROW_TILE):
    G, M, S = o_nsa.shape
    D = w_out.shape[1]
    hq = N_HEADS * HEAD_DIM
    wa = w_out[:hq].reshape(G, hq // G, D)
    wa = jnp.pad(wa, ((0, 0), (0, S - hq // G), (0, 0))).astype(BF16)
    wm = w_out[hq:].astype(BF16)
    return pl.pallas_call(
        _dense_ln_b_body,
        grid=(M // tm,),
        in_specs=[pl.BlockSpec((G, tm, S), lambda i: (0, i, 0)),
                  pl.BlockSpec((tm, MEM_W), lambda i: (i, 0)),
                  pl.BlockSpec((G, S, D), lambda i: (0, 0, 0)),
                  pl.BlockSpec((MEM_W, D), lambda i: (0, 0)),
                  pl.BlockSpec((tm, D), lambda i: (i, 0)),
                  pl.BlockSpec((1, D), lambda i: (0, 0)),
                  pl.BlockSpec((1, D), lambda i: (0, 0))],
        out_specs=pl.BlockSpec((tm, D), lambda i: (i, 0)),
        out_shape=jax.ShapeDtypeStruct((M, D), F32),
        compiler_params=_params("parallel"),
        name="dense_post_norm_b",
    )(o_nsa, o_mem, wa, wm, x, g.reshape(1, D), b.reshape(1, D))


def _router_body(x_ref, w_ref, b_ref, o_ref):
    o_ref[...] = jnp.dot(x_ref[...], w_ref[...], preferred_element_type=F32,
                         precision=lax.Precision.HIGHEST) + b_ref[...]


def router_logits(x, w_r, b_r, tm=ROW_TILE):
    M, K = x.shape
    w = jnp.pad(w_r, ((0, 0), (0, LANES - N_EXPERTS)))
    b = jnp.pad(b_r, (0, LANES - N_EXPERTS)).reshape(1, LANES)
    out = pl.pallas_call(
        _router_body,
        grid=(M // tm,),
        in_specs=[pl.BlockSpec((tm, K), lambda i: (i, 0)),
                  pl.BlockSpec((K, LANES), lambda i: (0, 0)),
                  pl.BlockSpec((1, LANES), lambda i: (0, 0))],
        out_specs=pl.BlockSpec((tm, LANES), lambda i: (i, 0)),
        out_shape=jax.ShapeDtypeStruct((M, LANES), F32),
        compiler_params=_params("parallel"),
        name="router",
    )(x, w, b)
    return out[:, :N_EXPERTS]


def _experts_body(tile_ref, e_ref, lo_ref, hi_ref, first_ref, x_ref, wr_ref, wgu_ref, bgu_ref,
                  wdn_ref, bdn_ref, o_ref, wgu_bf, wdn_bf):
    i = pl.program_id(0)
    e = e_ref[i]
    e_prev = e_ref[jnp.maximum(i - 1, 0)]
    lo, hi = lo_ref[i], hi_ref[i]

    @pl.when((i == 0) | (e != e_prev))
    def _():
        wgu_bf[...] = wgu_ref[0].astype(BF16)
        wdn_bf[...] = wdn_ref[0].astype(BF16)

    @pl.when(hi > lo)
    def _():
        h = jnp.dot(x_ref[...].astype(BF16), wgu_bf[...], preferred_element_type=F32) + bgu_ref[0]
        gate = jnp.minimum(h[:, :D_EXPERT], SWIGLU_LIMIT)
        up = jnp.clip(h[:, D_EXPERT:], -SWIGLU_LIMIT, SWIGLU_LIMIT)
        act = (up + 1.0) * gate * jax.nn.sigmoid(SWIGLU_ALPHA * gate)
        y = jnp.dot(act.astype(BF16), wdn_bf[...], preferred_element_type=F32) + bdn_ref[0]
        row = lax.broadcasted_iota(jnp.int32, (MOE_BLK, 1), 0)
        mine = (row >= lo) & (row < hi)
        y = y * wr_ref[...]

        @pl.when(first_ref[i] == 1)
        def _():
            o_ref[...] = jnp.where(mine, y, 0.0)

        @pl.when(first_ref[i] == 0)
        def _():
            o_ref[...] = jnp.where(mine, y, o_ref[...])


def experts(x_sorted, w_sorted, visits, layer, w_gu, b_gu, w_dn, b_dn):
    R, D = x_sorted.shape
    n_vis = visits[0].shape[0]
    e0 = layer * N_EXPERTS
    grid_spec = pltpu.PrefetchScalarGridSpec(
        num_scalar_prefetch=5,
        grid=(n_vis,),
        in_specs=[
            pl.BlockSpec((MOE_BLK, D), lambda i, t, e, lo, hi, f: (t[i], 0)),
            pl.BlockSpec((MOE_BLK, 1), lambda i, t, e, lo, hi, f: (t[i], 0)),
            pl.BlockSpec((1, D, 2 * D_EXPERT), lambda i, t, e, lo, hi, f: (e0 + e[i], 0, 0)),
            pl.BlockSpec((1, 1, 2 * D_EXPERT), lambda i, t, e, lo, hi, f: (e0 + e[i], 0, 0)),
            pl.BlockSpec((1, D_EXPERT, D), lambda i, t, e, lo, hi, f: (e0 + e[i], 0, 0)),
            pl.BlockSpec((1, 1, D), lambda i, t, e, lo, hi, f: (e0 + e[i], 0, 0)),
        ],
        out_specs=pl.BlockSpec((MOE_BLK, D), lambda i, t, e, lo, hi, f: (t[i], 0)),
        scratch_shapes=[pltpu.VMEM((D, 2 * D_EXPERT), BF16), pltpu.VMEM((D_EXPERT, D), BF16)],
    )
    return pl.pallas_call(
        _experts_body,
        grid_spec=grid_spec,
        out_shape=jax.ShapeDtypeStruct((R, D), F32),
        compiler_params=_params("arbitrary"),
        name="experts",
    )(*visits, x_sorted, w_sorted, w_gu.reshape((-1,) + w_gu.shape[-2:]),
      b_gu.reshape(-1, 1, b_gu.shape[-1]), w_dn.reshape((-1,) + w_dn.shape[-2:]),
      b_dn.reshape(-1, 1, b_dn.shape[-1]))


def _combine_ln_body(y_ref, x_ref, g_ref, b_ref, o_ref):
    y = y_ref[0]
    for k in range(1, TOP_K):
        y = y + y_ref[k]
    o_ref[...] = _layer_norm_rows(DN_ALPHA * x_ref[...] + y, g_ref[...], b_ref[...])


def combine_post_norm(y_assign, x, g, b, tm=ROW_TILE):
    M, D = x.shape
    return pl.pallas_call(
        _combine_ln_body,
        grid=(M // tm,),
        in_specs=[pl.BlockSpec((TOP_K, tm, D), lambda i: (0, i, 0)),
                  pl.BlockSpec((tm, D), lambda i: (i, 0)),
                  pl.BlockSpec((1, D), lambda i: (0, 0)),
                  pl.BlockSpec((1, D), lambda i: (0, 0))],
        out_specs=pl.BlockSpec((tm, D), lambda i: (i, 0)),
        out_shape=jax.ShapeDtypeStruct((M, D), F32),
        compiler_params=_params("parallel"),
        name="combine_post_norm",
    )(y_assign, x, g.reshape(1, D), b.reshape(1, D))


def expert_visits(starts, ends, n_tiles):
    n_vis = n_tiles + N_EXPERTS
    sizes = ends - starts
    first_tile = starts // MOE_BLK
    n_e = jnp.where(sizes > 0, (ends - 1) // MOE_BLK - first_tile + 1, 0)
    v_end = jnp.cumsum(n_e)
    v_start = v_end - n_e
    total = v_end[-1]
    v = jnp.arange(n_vis)
    real = v < total
    vc = jnp.minimum(v, total - 1)
    e_v = jnp.minimum(jnp.sum(v_end[None, :] <= vc[:, None], axis=1), N_EXPERTS - 1)
    tile_v = first_tile[e_v] + vc - v_start[e_v]
    lo = jnp.clip(starts[e_v] - tile_v * MOE_BLK, 0, MOE_BLK)
    hi = jnp.where(real, jnp.clip(ends[e_v] - tile_v * MOE_BLK, 0, MOE_BLK), lo)
    first = real & ((v == 0) | (tile_v != jnp.roll(tile_v, 1)))
    return tuple(a.astype(jnp.int32) for a in (tile_v, e_v, lo, hi, first))


def moe_post_norm(x, layer, w_r, b_r, w_gu, b_gu, w_dn, b_dn, g, b):
    N, D = x.shape
    logits = router_logits(x, w_r, b_r)
    top_v, top_e = lax.top_k(logits, TOP_K)
    wts = jax.nn.softmax(top_v, axis=-1)
    NK = N * TOP_K
    ids = jnp.arange(NK, dtype=jnp.int32)
    e_sorted, order, w_sorted = lax.sort(
        (top_e.reshape(-1).astype(jnp.int32), ids, wts.reshape(-1)), num_keys=1, is_stable=True)
    _, pos = lax.sort((order, ids), num_keys=1)
    ends = jnp.sum(e_sorted[None, :] <= jnp.arange(N_EXPERTS, dtype=jnp.int32)[:, None],
                   axis=1, dtype=jnp.int32)
    starts = jnp.concatenate([jnp.zeros((1,), ends.dtype), ends[:-1]])
    visits = expert_visits(starts, ends, NK // MOE_BLK)
    y_sorted = experts(x[order // TOP_K], w_sorted.reshape(NK, 1), visits, layer,
                       w_gu, b_gu, w_dn, b_dn)
    y_k = y_sorted[pos.reshape(N, TOP_K).T.reshape(-1)].reshape(TOP_K, N, D)
    return combine_post_norm(y_k, x, g, b)


NSA_TQ = 256
NSA_TK = 512
NSA_TW = 256
HEADS_PER_KV = N_HEADS // N_KV
MASK_OFF = -(2.0 ** 30)
SLAB = 2 * LANES


def _flash_step(s, v, rows, m_ref, l_ref, acc_ref):
    m_old = m_ref[rows, :]
    m_new = jnp.maximum(m_old, jnp.max(s, axis=1, keepdims=True))
    alpha = jnp.exp(m_old - m_new)
    p = jnp.exp(s - m_new)
    l_ref[rows, :] = alpha * l_ref[rows, :] + jnp.sum(p, axis=1, keepdims=True)
    acc_ref[rows, :] = alpha * acc_ref[rows, :] + jnp.dot(p.astype(BF16), v,
                                                          preferred_element_type=F32)
    m_ref[rows, :] = m_new


def _nt_dot(a, b):
    return lax.dot_general(a, b, (((1,), (1,)), ((), ())), preferred_element_type=F32)


def _nsa_prompt_body(p_ref, ks_ref, kw_ref, et_ref, kc_ref, covt_ref, o_ref,
                     qaug_ref, m_ref, l_ref, acc_ref, *, n_cmp, n_sel):
    TQ, R = NSA_TQ, HEADS_PER_KV
    i = pl.program_id(2)
    q0 = i * TQ
    slab = p_ref[...]
    a, bv = slab[:, :LANES], slab[:, LANES:]
    lane = lax.broadcasted_iota(jnp.int32, (TQ, LANES), 1)
    lo = lane < HEAD_DIM
    scale = HEAD_DIM ** -0.5
    q_heads = (jnp.where(lo, a, 0.0), jnp.where(lo, pltpu.roll(a, HEAD_DIM, 1), 0.0),
               jnp.where(lo, bv, 0.0))
    q3 = (jnp.concatenate(q_heads, axis=0) * scale).astype(BF16)
    row_q = q0 + lax.broadcasted_iota(jnp.int32, (TQ, 1), 0)
    row_q3 = jnp.concatenate([row_q] * R, axis=0)

    kc = kc_ref[0, 0]
    nc = kc.shape[0]
    s = _nt_dot(q3, kc)
    n_idx = lax.broadcasted_iota(jnp.int32, (R * TQ, nc), 1)
    vis = (n_idx * CMP_STRIDE + (CMP_LEN - 1) <= row_q3) & (n_idx < n_cmp)
    s = jnp.where(vis, s, MASK_OFF)
    m = jnp.max(s, axis=1, keepdims=True)
    p = jnp.where(vis, jnp.exp(s - m), 0.0)
    d = jnp.sum(p, axis=1, keepdims=True)
    p_c = p / jnp.where(d > 0, d, 1.0)
    o_c = jnp.dot(p_c.astype(BF16), kc, preferred_element_type=F32)

    p_sum = p_c[0:TQ]
    for r in range(1, R):
        p_sum = p_sum + p_c[r * TQ:(r + 1) * TQ]
    p_hi = p_sum.astype(BF16)
    p_lo = (p_sum - p_hi.astype(F32)).astype(BF16)
    covt = covt_ref[...]
    imp = _nt_dot(covt, p_hi) + _nt_dot(covt, p_lo)
    j = lax.broadcasted_iota(jnp.int32, (n_sel, TQ), 0)
    cur = jnp.right_shift(q0 + lax.broadcasted_iota(jnp.int32, (n_sel, TQ), 1),
                          SEL_BLK.bit_length() - 1)
    forced = (j == 0) | (j == cur) | (j == cur - 1)
    imp = jnp.where(forced, jnp.inf, jnp.where(j > cur, -jnp.inf, imp))
    rank = jnp.zeros((n_sel, TQ), F32)
    for b in range(n_sel):
        row = imp[b:b + 1, :]
        ahead = (row > imp) | ((row == imp) & (j > b))
        rank = rank + jnp.where(ahead, 1.0, 0.0)
    sel_off = jnp.where(rank < float(min(N_SEL, n_sel)), 0.0, MASK_OFF)
    sel_off = jnp.concatenate([sel_off, jnp.zeros((LANES - n_sel, TQ), F32)], axis=0)
    mask_q = sel_off.T.astype(BF16)
    qaug_ref[:, :LANES] = q3
    qaug_ref[:, LANES:] = jnp.concatenate([mask_q] * R, axis=0)

    def reset():
        m_ref[...] = jnp.full(m_ref.shape, MASK_OFF, F32)
        l_ref[...] = jnp.zeros(l_ref.shape, F32)
        acc_ref[...] = jnp.zeros(acc_ref.shape, F32)

    def finish():
        return acc_ref[...] / l_ref[...]

    reset()
    n_full = q0 // NSA_TK

    def sel_tile(t, masked):
        k0 = pl.multiple_of(t * NSA_TK, NSA_TK)
        kt = ks_ref[pl.ds(k0, NSA_TK), :]
        kcat = jnp.concatenate([kt, et_ref[pl.ds(k0, NSA_TK), :]], axis=1)
        if masked:
            causal = k0 + lax.broadcasted_iota(jnp.int32, (TQ, NSA_TK), 1) <= row_q
        for r in range(R):
            rows = pl.ds(r * TQ, TQ)
            s = _nt_dot(qaug_ref[rows, :], kcat)
            if masked:
                s = jnp.where(causal, s, MASK_OFF)
            _flash_step(s, kt, rows, m_ref, l_ref, acc_ref)

    def sel_loop(t, c):
        sel_tile(t, False)
        return c

    lax.fori_loop(0, n_full, sel_loop, 0)
    sel_tile(n_full, True)
    o_s = finish()

    band = WINDOW + TQ
    w0 = pl.multiple_of(jnp.maximum(q0 - WINDOW, 0), NSA_TW)
    kt = kw_ref[pl.ds(w0, band), :]
    dist = row_q - (w0 + lax.broadcasted_iota(jnp.int32, (TQ, band), 1))
    inside = (dist >= 0) & (dist <= WINDOW)
    o_w = []
    for r in range(R):
        s = jnp.where(inside, _nt_dot(qaug_ref[pl.ds(r * TQ, TQ), :LANES], kt), MASK_OFF)
        p = jnp.exp(s - jnp.max(s, axis=1, keepdims=True))
        o_w.append(jnp.dot(p.astype(BF16), kt, preferred_element_type=F32)
                   / jnp.sum(p, axis=1, keepdims=True))
    o_w = jnp.concatenate(o_w, axis=0)

    gates = jax.nn.sigmoid(bv)
    outs = []
    for r in range(R):
        c0 = HEAD_DIM + 3 * r
        rows = slice(r * TQ, (r + 1) * TQ)
        outs.append(gates[:, c0:c0 + 1] * o_c[rows] + gates[:, c0 + 1:c0 + 2] * o_s[rows]
                    + gates[:, c0 + 2:c0 + 3] * o_w[rows])
    o_ref[0, :, :LANES] = jnp.where(lo, pltpu.roll(outs[0], HEAD_DIM, 1), outs[1])
    o_ref[0, :, LANES:] = jnp.where(lo, pltpu.roll(outs[2], HEAD_DIM, 1), 0.0)


def nsa_prompt_pallas(p, ksw, kc, covt, B, T, n_cmp):
    n_sel = covt.shape[0]
    assert T % NSA_TK == 0 and NSA_TK % NSA_TQ == 0 and T >= WINDOW + NSA_TQ
    nq = T // NSA_TQ
    rows = HEADS_PER_KV * NSA_TQ
    e_t = (jnp.arange(T)[:, None] // SEL_BLK == jnp.arange(LANES)[None, :]).astype(BF16)
    body = functools.partial(_nsa_prompt_body, n_cmp=n_cmp, n_sel=n_sel)
    return pl.pallas_call(
        body,
        grid=(B, N_KV, nq),
        in_specs=[pl.BlockSpec((NSA_TQ, SLAB), lambda b, g, i: (b * nq + i, g)),
                  pl.BlockSpec((T, LANES), lambda b, g, i: (b, g)),
                  pl.BlockSpec((T, LANES), lambda b, g, i: (b, N_KV + g)),
                  pl.BlockSpec((T, LANES), lambda b, g, i: (0, 0)),
                  pl.BlockSpec((1, 1) + kc.shape[2:], lambda b, g, i: (b, g, 0, 0)),
                  pl.BlockSpec(covt.shape, lambda b, g, i: (0, 0))],
        out_specs=pl.BlockSpec((1, NSA_TQ, SLAB), lambda b, g, i: (g, b * nq + i, 0)),
        out_shape=jax.ShapeDtypeStruct((N_KV, B * T, SLAB), F32),
        scratch_shapes=[pltpu.VMEM((rows, SLAB), BF16), pltpu.VMEM((rows, 1), F32),
                        pltpu.VMEM((rows, 1), F32), pltpu.VMEM((rows, LANES), F32)],
        compiler_params=_params("parallel", "parallel", "arbitrary"),
        name="nsa_prompt",
    )(p, ksw, ksw, e_t, kc, covt)


def nsa_kv_weight(w_kv):
    D = w_kv.shape[0]
    w = w_kv.reshape(D, 6, N_KV, HEAD_DIM)
    pair = lambda c: w[:, c:c + 2].transpose(0, 2, 1, 3).reshape(D, N_KV * 2 * HEAD_DIM)
    return jnp.concatenate([pair(2), pair(4)], axis=1).astype(BF16)


def nsa_prompt_operands(kc_all, B, T):
    n_sel = T // SEL_BLK
    n_chunk = T // CMP_STRIDE
    n_cmp = n_chunk - CMP_R + 1
    kcv = kc_all.reshape(2, B, n_chunk, N_KV, HEAD_DIM).transpose(1, 3, 2, 0, 4)
    kcv = kcv.reshape(B, N_KV, n_chunk, 2 * HEAD_DIM).astype(BF16)
    c_start = jnp.arange(n_chunk) * CMP_STRIDE
    s_start = jnp.arange(n_sel) * SEL_BLK
    covt = ((c_start[None, :] < s_start[:, None] + SEL_BLK)
            & (c_start[None, :] + CMP_LEN - 1 >= s_start[:, None])
            & (jnp.arange(n_chunk)[None, :] < n_cmp)).astype(BF16)
    return kcv, covt


CMP_R = CMP_LEN // CMP_STRIDE
KV_W = N_KV * HEAD_DIM


def _compress_body(xa_ref, xb_ref, w1_ref, pe_ref, b1_ref, w2_ref, b2_ref, o_ref):
    m = o_ref.shape[1]
    acc = [None] * CMP_R
    for s in range(CMP_STRIDE):
        rows_s = pl.ds(s, m, stride=CMP_STRIDE)
        xs = jnp.concatenate([xa_ref[rows_s, :], xb_ref[rows_s, :]], axis=1)
        for r in range(CMP_R):
            t = jnp.dot((xs + pe_ref[0, r, s:s + 1, :]).astype(BF16), w1_ref[0, r, s],
                        preferred_element_type=F32)
            acc[r] = t if acc[r] is None else acc[r] + t
    hid = b1_ref[0] + acc[0]
    for r in range(1, CMP_R):
        hid = hid + pltpu.roll(acc[r], m - r, 0)
    o_ref[0] = jnp.dot(jax.nn.gelu(hid).astype(BF16), w2_ref[0],
                       preferred_element_type=F32) + b2_ref[0]


def compress_operands(cmp_pe, cmp_w1, cmp_b1, cmp_w2, cmp_b2):
    eye = jnp.eye(N_KV, dtype=F32)
    w1 = cmp_w1.reshape(2, CMP_R, CMP_STRIDE, HEAD_DIM, CMP_HID)
    w1 = jnp.einsum('krsdh,gG->krsgdGh', w1, eye).reshape(
        2, CMP_R, CMP_STRIDE, KV_W, N_KV * CMP_HID).astype(BF16)
    pe = jnp.tile(cmp_pe.reshape(2, CMP_R, CMP_STRIDE, HEAD_DIM), (1, 1, 1, N_KV))
    b1 = jnp.tile(cmp_b1, (1, N_KV)).reshape(2, 1, N_KV * CMP_HID)
    w2 = jnp.einsum('khd,gG->kghGd', cmp_w2, eye).reshape(2, N_KV * CMP_HID, KV_W).astype(BF16)
    b2 = jnp.tile(cmp_b2, (1, N_KV)).reshape(2, 1, KV_W)
    return w1, pe, b1, w2, b2


def compress(x2d, n_seq, seq_rows, seqs_per_step, ops):
    w1, pe, b1, w2, b2 = ops
    rows = seqs_per_step * seq_rows
    m = rows // CMP_STRIDE
    return pl.pallas_call(
        _compress_body,
        grid=(2, n_seq // seqs_per_step),
        in_specs=[pl.BlockSpec((rows, LANES), lambda k, i: (i, 2 * k)),
                  pl.BlockSpec((rows, LANES), lambda k, i: (i, 2 * k + 1)),
                  pl.BlockSpec((1,) + w1.shape[1:], lambda k, i: (k, 0, 0, 0, 0)),
                  pl.BlockSpec((1,) + pe.shape[1:], lambda k, i: (k, 0, 0, 0)),
                  pl.BlockSpec((1,) + b1.shape[1:], lambda k, i: (k, 0, 0)),
                  pl.BlockSpec((1,) + w2.shape[1:], lambda k, i: (k, 0, 0)),
                  pl.BlockSpec((1,) + b2.shape[1:], lambda k, i: (k, 0, 0))],
        out_specs=pl.BlockSpec((1, m, KV_W), lambda k, i: (k, i, 0)),
        out_shape=jax.ShapeDtypeStruct((2, n_seq * seq_rows // CMP_STRIDE, KV_W), F32),
        compiler_params=_params("arbitrary", "arbitrary"),
        name="compress",
    )(x2d, x2d, w1, pe, b1, w2, b2)


def _softmax_two(s_a, s_b):
    m = jnp.maximum(jnp.max(s_a, axis=1, keepdims=True), jnp.max(s_b, axis=1, keepdims=True))
    p_a, p_b = jnp.exp(s_a - m), jnp.exp(s_b - m)
    d = jnp.sum(p_a, axis=1, keepdims=True) + jnp.sum(p_b, axis=1, keepdims=True)
    return p_a, p_b, d


def _nsa_sample_body(p_ref, past_ref, new_ref, win_ref, kc_ref, cov_ref, esel_ref, o_ref,
                     *, past_len, n_cmp, n_sel):
    S = p_ref.shape[0]
    G, R = N_KV, HEADS_PER_KV
    w_buf = win_ref.shape[0]
    slab = p_ref[...]
    lane = lax.broadcasted_iota(jnp.int32, (S, LANES), 1)
    lo = lane < HEAD_DIM
    zero = jnp.zeros((S, LANES), F32)

    def half(x, src_hi, dst_hi):
        x = x if src_hi == dst_hi else pltpu.roll(x, HEAD_DIM, 1)
        return jnp.where(lo != dst_hi, x, 0.0)

    q_rows, gate_cols = [], ([], [], [])
    for g in range(G):
        a = slab[:, g * SLAB:g * SLAB + LANES]
        bv = slab[:, g * SLAB + LANES:(g + 1) * SLAB]
        sig = jax.nn.sigmoid(bv)
        for r in range(R):
            piece = half(a if r < 2 else bv, r == 1, g % 2 == 1)
            tiles = [zero, zero]
            tiles[g // 2] = piece
            q_rows.append(jnp.concatenate(tiles, axis=1))
            for br in range(3):
                c = HEAD_DIM + 3 * r + br
                gate_cols[br].append(sig[:, c:c + 1])
    q_all = (jnp.concatenate(q_rows, axis=0) * HEAD_DIM ** -0.5).astype(BF16)
    gate = [jnp.concatenate(cols, axis=0) for cols in gate_cols]
    n_rows = G * R * S
    t_col = lax.broadcasted_iota(jnp.int32, (S, 1), 0)
    t_row = jnp.concatenate([t_col] * (G * R), axis=0)
    q_pos = past_len + t_row

    kck = kc_ref[0].astype(BF16)
    kcv = kc_ref[1].astype(BF16)
    nc = kck.shape[0]
    s = _nt_dot(q_all, kck)
    n_idx = lax.broadcasted_iota(jnp.int32, (n_rows, nc), 1)
    vis = (n_idx * CMP_STRIDE + (CMP_LEN - 1) <= q_pos) & (n_idx < n_cmp)
    s = jnp.where(vis, s, MASK_OFF)
    m = jnp.max(s, axis=1, keepdims=True)
    p = jnp.where(vis, jnp.exp(s - m), 0.0)
    d = jnp.sum(p, axis=1, keepdims=True)
    p_c = p / jnp.where(d > 0, d, 1.0)
    o_c = jnp.dot(p_c.astype(BF16), kcv, preferred_element_type=F32)

    p_sum = []
    for g in range(G):
        acc = p_c[g * R * S:(g * R + 1) * S]
        for r in range(1, R):
            acc = acc + p_c[(g * R + r) * S:(g * R + r + 1) * S]
        p_sum.append(acc)
    p_sum = jnp.concatenate(p_sum, axis=0)
    p_hi = p_sum.astype(BF16)
    p_lo = (p_sum - p_hi.astype(F32)).astype(BF16)
    cov = cov_ref[...]
    imp = (jnp.dot(p_hi, cov, preferred_element_type=F32)
           + jnp.dot(p_lo, cov, preferred_element_type=F32))
    j = lax.broadcasted_iota(jnp.int32, (G * S, LANES), 1)
    cur = jnp.right_shift(past_len + jnp.concatenate([t_col] * G, axis=0), SEL_BLK.bit_length() - 1)
    forced = (j == 0) | (j == cur) | (j == cur - 1)
    imp = jnp.where(forced, jnp.inf, jnp.where((j > cur) | (j >= n_sel), -jnp.inf, imp))
    rank = jnp.zeros((G * S, LANES), F32)
    for b in range(n_sel):
        col = imp[:, b:b + 1]
        rank = rank + jnp.where((col > imp) | ((col == imp) & (j > b)), 1.0, 0.0)
    sel_off = jnp.where(rank < float(min(N_SEL, n_sel)), 0.0, MASK_OFF)
    mask_rows = jnp.concatenate(
        [sel_off[g * S:(g + 1) * S] for g in range(G) for _ in range(R)], axis=0).astype(BF16)

    new = new_ref[...]
    pad = jnp.zeros((LANES - S, KV_W), F32)
    u = lax.broadcasted_iota(jnp.int32, (n_rows, LANES), 1)
    new_vis = u <= t_row

    def new_keys(slot):
        return jnp.concatenate([new[:, slot * KV_W:(slot + 1) * KV_W], pad], axis=0).astype(BF16)

    esel = esel_ref[...]
    s_p = (_nt_dot(q_all, past_ref[:, :KV_W].astype(BF16))
           + jnp.dot(mask_rows, esel[:, :past_len], preferred_element_type=F32))
    s_n = (_nt_dot(q_all, new_keys(2))
           + jnp.dot(mask_rows, esel[:, past_len:], preferred_element_type=F32))
    s_n = jnp.where(new_vis, s_n, MASK_OFF)
    p_p, p_n, d = _softmax_two(s_p, s_n)
    o_s = (jnp.dot(p_p.astype(BF16), past_ref[:, KV_W:].astype(BF16), preferred_element_type=F32)
           + jnp.dot(p_n.astype(BF16), new_keys(3), preferred_element_type=F32)) / d

    s_w = _nt_dot(q_all, win_ref[:, :KV_W].astype(BF16))
    w_pos = (past_len - w_buf) + lax.broadcasted_iota(jnp.int32, (n_rows, w_buf), 1)
    dist = q_pos - w_pos
    s_w = jnp.where((w_pos >= 0) & (dist >= 0) & (dist <= WINDOW), s_w, MASK_OFF)
    s_wn = jnp.where(new_vis, _nt_dot(q_all, new_keys(4)), MASK_OFF)
    p_w, p_wn, d = _softmax_two(s_w, s_wn)
    o_w = (jnp.dot(p_w.astype(BF16), win_ref[:, KV_W:].astype(BF16), preferred_element_type=F32)
           + jnp.dot(p_wn.astype(BF16), new_keys(5), preferred_element_type=F32)) / d

    out = gate[0] * o_c + gate[1] * o_s + gate[2] * o_w

    for g in range(G):
        def head(r, dst_hi):
            rows = out[(g * R + r) * S:(g * R + r + 1) * S]
            return half(rows[:, (g // 2) * LANES:(g // 2 + 1) * LANES], g % 2 == 1, dst_hi)
        o_ref[:, g * SLAB:g * SLAB + LANES] = head(0, False) + head(1, True)
        o_ref[:, g * SLAB + LANES:(g + 1) * SLAB] = head(2, False)


def nsa_sample_pallas(p, row0, past2d, kv, cache_win2d, kc_all, Bs, S, past_len):
    w_buf = cache_win2d.shape[0] // Bs
    T = past_len + S
    n_sel = -(-T // SEL_BLK)
    nc = kc_all.shape[1] // Bs
    n_cmp = T // CMP_STRIDE - CMP_R + 1
    c_start = jnp.arange(nc) * CMP_STRIDE
    s_start = jnp.arange(LANES) * SEL_BLK
    cov = ((c_start[:, None] < s_start[None, :] + SEL_BLK)
           & (c_start[:, None] + CMP_LEN - 1 >= s_start[None, :])
           & (jnp.arange(nc)[:, None] < n_cmp) & (jnp.arange(LANES)[None, :] < n_sel)).astype(BF16)
    k_pos = jnp.arange(past_len + LANES)
    esel = ((k_pos[None, :] // SEL_BLK == jnp.arange(LANES)[:, None])
            & (k_pos[None, :] < T)).astype(BF16)
    body = functools.partial(_nsa_sample_body, past_len=past_len, n_cmp=n_cmp, n_sel=n_sel)
    blk0 = row0 // S
    return pl.pallas_call(
        body,
        grid=(Bs,),
        in_specs=[pl.BlockSpec((S, N_KV * SLAB), lambda b: (blk0 + b, 0)),
                  pl.BlockSpec((past_len, 2 * KV_W), lambda b: (b, 1)),
                  pl.BlockSpec((S, 6 * KV_W), lambda b: (blk0 + b, 0)),
                  pl.BlockSpec((w_buf, 2 * KV_W), lambda b: (b, 0)),
                  pl.BlockSpec((2, nc, KV_W), lambda b: (0, b, 0)),
                  pl.BlockSpec(cov.shape, lambda b: (0, 0)),
                  pl.BlockSpec(esel.shape, lambda b: (0, 0))],
        out_specs=pl.BlockSpec((S, N_KV * SLAB), lambda b: (b, 0)),
        out_shape=jax.ShapeDtypeStruct((Bs * S, N_KV * SLAB), F32),
        compiler_params=_params("parallel"),
        name="nsa_sample",
    )(p, past2d, kv, cache_win2d, kc_all, cov, esel)


RG_TB = 256


def _rglru_body(*refs, seg, carry):
    if carry:
        (gate_ref, u_ref, cw_ref, cb_ref, wai_ref, bai_ref, sp_ref,
         y_ref, hl_ref, prev_ref, h_ref) = refs
    else:
        (gate_ref, u_ref, a1_ref, a2_ref, a3_ref, h0_ref, cw_ref, cb_ref, wai_ref, bai_ref,
         sp_ref, y_ref, hall_ref) = refs
        alt_refs = (a1_ref, a2_ref, a3_ref)
    TB = u_ref.shape[0]
    u = u_ref[...]
    t = lax.broadcasted_iota(jnp.int32, (TB, 1), 0) & (seg - 1)
    if carry:
        @pl.when(pl.program_id(1) == 0)
        def _():
            prev_ref[...] = jnp.zeros(prev_ref.shape, F32)
            h_ref[...] = jnp.zeros(h_ref.shape, F32)
        prev = prev_ref[...]

    xc = u * cw_ref[CONV_W - 1:CONV_W, :] + cb_ref[...]
    for k in range(1, CONV_W):
        back = pltpu.roll(prev, k, 0) if carry else alt_refs[k - 1][...]
        xc = xc + jnp.where(t >= k, pltpu.roll(u, k, 0), back) * cw_ref[CONV_W - 1 - k:CONV_W - k, :]

    ri = jnp.dot(xc.astype(BF16), wai_ref[...], preferred_element_type=F32) + bai_ref[...]
    r = jax.nn.sigmoid(ri[:, :D_RNN])
    i_gate = jax.nn.sigmoid(ri[:, D_RNN:])
    log_a = -RG_C * r * sp_ref[...]
    a = jnp.exp(log_a)
    x2 = 2.0 * log_a
    e2 = jnp.exp(x2)
    one = e2 == 1.0
    expm1 = jnp.where(one, x2, (e2 - 1.0) * x2 / jnp.where(one, 1.0, jnp.log(e2)))
    b = jnp.sqrt(-expm1) * (i_gate * xc)

    d = 1
    while d < seg:
        ok = t >= d
        b = jnp.where(ok, a * pltpu.roll(b, d, 0) + b, b)
        a = jnp.where(ok, a * pltpu.roll(a, d, 0), a)
        d *= 2
    h = a * (h_ref[0:1, :] if carry else h0_ref[...]) + b
    y_ref[...] = h * jax.nn.gelu(gate_ref[...])
    if carry:
        prev_ref[...] = u
        h_last = h[TB - 1:TB, :]
        h_ref[...] = jnp.broadcast_to(h_last, h_ref.shape)
        hl_ref[0] = h_last
    else:
        hall_ref[...] = h


def rglru_operands(conv_w, conv_b, w_a, b_a, w_i, b_i, lam):
    eye = jnp.eye(N_RG_BLOCKS, dtype=F32)
    bd = lambda w: jnp.einsum('ncd,nm->ncmd', w, eye).reshape(D_RNN, D_RNN)
    wai = jnp.concatenate([bd(w_a), bd(w_i)], axis=1).astype(BF16)
    bai = jnp.concatenate([b_a, b_i]).reshape(1, 2 * D_RNN)
    sp = jax.nn.softplus(-lam.astype(F32)).reshape(1, D_RNN)
    return conv_w, conv_b.reshape(1, D_RNN), wai, bai, sp


def _rg_param_specs():
    z = (lambda *_: (0, 0))
    return [pl.BlockSpec((CONV_W, D_RNN), z), pl.BlockSpec((1, D_RNN), z),
            pl.BlockSpec((D_RNN, 2 * D_RNN), z), pl.BlockSpec((1, 2 * D_RNN), z),
            pl.BlockSpec((1, D_RNN), z)]


def rglru_prompt(p, B, T, ops):
    nblk = T // RG_TB
    body = functools.partial(_rglru_body, seg=RG_TB, carry=True)
    return pl.pallas_call(
        body,
        grid=(B, nblk),
        in_specs=[pl.BlockSpec((RG_TB, D_RNN), lambda b, i: (b * nblk + i, 0)),
                  pl.BlockSpec((RG_TB, D_RNN), lambda b, i: (b * nblk + i, 1))] + _rg_param_specs(),
        out_specs=[pl.BlockSpec((RG_TB, D_RNN), lambda b, i: (b * nblk + i, 0)),
                   pl.BlockSpec((1, 1, D_RNN), lambda b, i: (b, 0, 0))],
        out_shape=[jax.ShapeDtypeStruct((B * T, D_RNN), F32),
                   jax.ShapeDtypeStruct((B, 1, D_RNN), F32)],
        scratch_shapes=[pltpu.VMEM((RG_TB, D_RNN), F32), pltpu.VMEM((8, D_RNN), F32)],
        compiler_params=_params("parallel", "arbitrary"),
        name="rglru_prompt",
    )(p, p, *ops)


def rglru_sample(p, row0, B, S, conv_buf, h0, ops, tb=512):
    ext = jnp.concatenate([conv_buf, jnp.zeros((B, S, D_RNN), F32)], axis=1)
    alts = [ext[:, CONV_W - 1 - k:CONV_W - 1 - k + S].reshape(B * S, D_RNN)
            for k in range(1, CONV_W)]
    h0x = jnp.repeat(h0, S, axis=0)
    blk0 = row0 // tb
    body = functools.partial(_rglru_body, seg=S, carry=False)
    row = lambda i: (i, 0)
    return pl.pallas_call(
        body,
        grid=(B * S // tb,),
        in_specs=[pl.BlockSpec((tb, D_RNN), lambda i: (blk0 + i, 0)),
                  pl.BlockSpec((tb, D_RNN), lambda i: (blk0 + i, 1))]
                 + [pl.BlockSpec((tb, D_RNN), row)] * 4 + _rg_param_specs(),
        out_specs=[pl.BlockSpec((tb, D_RNN), row), pl.BlockSpec((tb, D_RNN), row)],
        out_shape=[jax.ShapeDtypeStruct((B * S, D_RNN), F32)] * 2,
        compiler_params=_params("parallel"),
        name="rglru_sample",
    )(p, p, *alts, h0x, *ops)


def _mem_attn_body(q_ref, kv_ref, o_ref):
    TB = q_ref.shape[0]
    q = q_ref[...] * MEM_HD ** -0.5
    head = jnp.right_shift(lax.broadcasted_iota(jnp.int32, (TB, MEM_W), 1), MEM_HD.bit_length() - 1)
    q_all = jnp.concatenate([jnp.where(head == h, q, 0.0) for h in range(MEM_HEADS)],
                            axis=0).astype(BF16)
    k = kv_ref[:, :MEM_W].astype(BF16)
    v = kv_ref[:, MEM_W:].astype(BF16)
    s = _nt_dot(q_all, k)
    p = jnp.exp(s - jnp.max(s, axis=1, keepdims=True))
    o = jnp.dot(p.astype(BF16), v, preferred_element_type=F32) / jnp.sum(p, axis=1, keepdims=True)
    out = jnp.where(head == 0, o[0:TB], 0.0)
    for h in range(1, MEM_HEADS):
        out = out + jnp.where(head == h, o[h * TB:(h + 1) * TB], 0.0)
    o_ref[...] = out


def mem_attend_pallas(p, row0, q_col_blk, n_seq, seq_rows, tb, kv2d, kv_blk0, kv_col_blk):
    nblk = seq_rows // tb
    blk0 = row0 // tb
    return pl.pallas_call(
        _mem_attn_body,
        grid=(n_seq, nblk),
        in_specs=[pl.BlockSpec((tb, MEM_W), lambda b, i: (blk0 + b * nblk + i, q_col_blk)),
                  pl.BlockSpec((N_MEM, 2 * MEM_W), lambda b, i: (kv_blk0 + b, kv_col_blk))],
        out_specs=pl.BlockSpec((tb, MEM_W), lambda b, i: (b * nblk + i, 0)),
        out_shape=jax.ShapeDtypeStruct((n_seq * seq_rows, MEM_W), F32),
        compiler_params=_params("parallel", "parallel"),
        name="mem_attend",
    )(p, kv2d)


def kernel(x_prompt, x_sample, cache_kv, cache_win, cache_mem_kv, state_rg_h, state_rg_conv,
           page_table, mem_prompt, ln_g, ln_b, w_in_a, conv_w, conv_b, w_rg_a, b_rg_a,
           w_rg_i, b_rg_i, rg_lambda, w_out_a, w_kv_shared, cmp_pe, cmp_w1, cmp_b1, cmp_w2,
           cmp_b2, w_in_b, w_out_b, w_mem_kv, w_router, b_router, w_gate_up, b_gate_up,
           w_down, b_down):
    hq = N_HEADS * HEAD_DIM
    Bp, Tp, D = x_prompt.shape
    Bs, Ss, _ = x_sample.shape
    Np, Ns = Bp * Tp, Bs * Ss
    past_len = page_table.shape[1] * PAGE_SIZE
    w_buf = cache_win.shape[1]
    cmp_params = (cmp_pe, cmp_w1, cmp_b1, cmp_w2, cmp_b2)

    def split(t):
        return t[:Np].reshape(Bp, Tp, -1), t[Np:].reshape(Bs, Ss, -1)

    def join(tp, ts):
        return jnp.concatenate([tp.reshape(Np, -1), ts.reshape(Ns, -1)], axis=0)

    w_mem_all = w_mem_kv.transpose(1, 0, 2).reshape(D, DEPTH * 2 * MEM_W).astype(BF16)
    mem_all = dense(mem_prompt.reshape(Bp * N_MEM, D), w_mem_all, tm=Bp * N_MEM // 2)
    mem_kv_prompt = mem_all.reshape(Bp, N_MEM, DEPTH, 2, MEM_HEADS, MEM_HD).transpose(2, 0, 1, 3, 4, 5)
    mem_kv_sample = cache_mem_kv.reshape(DEPTH * Bs * N_MEM, 2 * MEM_W)

    def mem_attention(p, q_col_blk, l):
        o_p = mem_attend_pallas(p, 0, q_col_blk, Bp, Tp, ROW_TILE, mem_all, 0, l)
        o_s = mem_attend_pallas(p, Np, q_col_blk, Bs, Ss, Ss, mem_kv_sample, l * Bs, 0)
        return jnp.concatenate([o_p, o_s], axis=0)

    x = join(x_prompt, x_sample)
    hs_p, hs_s, bufs_p, bufs_s = [], [], [], []
    for l in range(DEPTH):
        if l < N_A:
            p = dense(x, w_in_a[l].astype(BF16))
            rg_ops = rglru_operands(conv_w[l], conv_b[l], w_rg_a[l], b_rg_a[l], w_rg_i[l],
                                    b_rg_i[l], rg_lambda[l])
            y_p, h_last_p = rglru_prompt(p, Bp, Tp, rg_ops)
            y_s, h_all_s = rglru_sample(p, Np, Bs, Ss, state_rg_conv[l], state_rg_h[l], rg_ops)
            hs_p.append(h_last_p.reshape(Bp, D_RNN))
            hs_s.append(h_all_s.reshape(Bs, Ss, D_RNN)[:, -1])
            tail = CONV_W - 1
            assert min(Tp, Ss) >= tail
            bufs_p.append(jnp.stack([p[(b + 1) * Tp - tail:(b + 1) * Tp, D_RNN:2 * D_RNN]
                                     for b in range(Bp)]))
            bufs_s.append(p[Np:, D_RNN:2 * D_RNN].reshape(Bs, Ss, D_RNN)[:, Ss - tail:])
            o_mem = mem_attention(p, 2 * D_RNN // MEM_W, l)
            x = dense_post_norm(jnp.concatenate([y_p, y_s], axis=0), o_mem, w_out_a[l], x,
                                ln_g[l, 0], ln_b[l, 0])
        else:
            jb = l - N_A
            if l == N_A:
                kv = dense(x, w_kv_shared.astype(BF16))
                cmp_ops = compress_operands(*cmp_params)
                kc_p = compress(kv, Bp, Tp, 1, cmp_ops)
                assert (past_len + Ss) // CMP_STRIDE * CMP_STRIDE == past_len
                past2d = cache_kv.reshape(-1, PAGE_SIZE, 4 * KV_W)[page_table].reshape(
                    Bs * past_len, 4 * KV_W)
                kc_s = compress(past2d, Bs, past_len, 2, cmp_ops)
                kcv, covt = nsa_prompt_operands(kc_p, Bp, Tp)
                ksw = dense(x, nsa_kv_weight(w_kv_shared), rows=Np, out_dtype=BF16)
            hpg = hq // N_KV
            wq = w_in_b[jb][:, :hq].reshape(D, N_KV, hpg)
            wg = w_in_b[jb][:, hq:hq + 3 * N_HEADS].reshape(D, N_KV, 3 * HEADS_PER_KV)
            slabs = jnp.concatenate(
                [wq, wg, jnp.zeros((D, N_KV, SLAB - hpg - 3 * HEADS_PER_KV), F32)], axis=-1)
            w_in = jnp.concatenate([slabs.reshape(D, N_KV * SLAB), w_in_b[jb][:, hq + 3 * N_HEADS:]],
                                   axis=1).astype(BF16)
            p = dense(x, w_in)
            o_nsa_p = nsa_prompt_pallas(p, ksw, kcv, covt, Bp, Tp,
                                        Tp // CMP_STRIDE - CMP_R + 1)
            o_nsa_s = nsa_sample_pallas(p, Np, past2d, kv, cache_win.reshape(Bs * w_buf, 2 * KV_W),
                                        kc_s, Bs, Ss, past_len)
            o_nsa = jnp.concatenate(
                [o_nsa_p, o_nsa_s.reshape(Ns, N_KV, SLAB).transpose(1, 0, 2)], axis=1)
            o_mem = mem_attention(p, N_KV * SLAB // MEM_W, l)
            x = dense_post_norm_b(o_nsa, o_mem, w_out_b[jb], x, ln_g[l, 0], ln_b[l, 0])
        x = moe_post_norm(x, l, w_router[l], b_router[l], w_gate_up, b_gate_up, w_down, b_down,
                          ln_g[l, 1], ln_b[l, 1])

    y_prompt, y_sample = split(x)
    slot = (N_KV, HEAD_DIM)
    kv_rows_prompt = kv[:Np, :4 * KV_W].reshape((Bp, Tp, 4) + slot)
    kv_rows_sample = kv[Np:, :4 * KV_W].reshape((Bs, Ss, 4) + slot)
    n_win = min(WINDOW, Tp)
    win_prompt = jnp.stack([kv[(b + 1) * Tp - n_win:(b + 1) * Tp, 4 * KV_W:]
                            for b in range(Bp)]).reshape((Bp, n_win, 2) + slot)
    win_s = jnp.concatenate([cache_win, kv[Np:, 4 * KV_W:].reshape((Bs, Ss, 2) + slot)], axis=1)
    win_sample = win_s[:, win_s.shape[1] - min(WINDOW, past_len + Ss):]
    return (y_prompt, y_sample, kv_rows_prompt, kv_rows_sample, win_prompt, win_sample,
            mem_kv_prompt, jnp.stack(hs_p), jnp.stack(hs_s), jnp.stack(bufs_p), jnp.stack(bufs_s))
```
